```python
import math
import jax, jax.numpy as jnp
from jax import lax
import numpy as np

D_MODEL = 2048
BATCH = 16
SEQ = 2048
DEPTH = 2

HEAD_DIM = 64
BLOCK = 128
LRU_WIDTH = 1024
LRU_BLOCKS = 16
LRU_CONV = 4
LRU_C = 8.0
SWA_HEADS = 16
SWA_KV_HEADS = 4
SWA_WINDOW = 128
S5_WIDTH = 768
S5_GROUP = 16
S5_GROUPS = S5_WIDTH // S5_GROUP
S5_STATE = 64
DIL_CONFIGS = ((128, 1), (512, 4), (2048, 16))
DIL_HEADS = 8
DIL_KV_HEADS = 4
D_FF = 5632
FFN_CONV = 3
ALPHA = (2 * DEPTH) ** 0.25
BETA = (8 * DEPTH) ** -0.25
LN_EPS = 1e-5

SWA_Q = SWA_HEADS * HEAD_DIM
SWA_KV = SWA_KV_HEADS * HEAD_DIM
DIL_Q = len(DIL_CONFIGS) * DIL_HEADS * HEAD_DIM
DIL_KV = DIL_KV_HEADS * HEAD_DIM
EVEN_IN = 2 * LRU_WIDTH + SWA_Q + 2 * SWA_KV
EVEN_OUT = LRU_WIDTH + SWA_Q
ODD_IN = S5_WIDTH + DIL_Q + 2 * DIL_KV
ODD_OUT = S5_WIDTH + DIL_HEADS * HEAD_DIM

kernel_name = 'hybrid_rglru_swa_s5_dilated_deepnorm'


def layer_norm(x, g, b):
    xf = x.astype(jnp.float32)
    mu = xf.mean(-1, keepdims=True)
    var = jnp.square(xf - mu).mean(-1, keepdims=True)
    y = (xf - mu) * lax.rsqrt(var + LN_EPS) * g.astype(jnp.float32) + b.astype(jnp.float32)
    return y.astype(x.dtype)


def causal_depthwise_conv(x, w, b):
    k, c = w.shape
    y = lax.conv_general_dilated(x, w[:, None, :].astype(x.dtype), window_strides=(1,),
                                 padding=((k - 1, 0),), dimension_numbers=('NWC', 'WIO', 'NWC'),
                                 feature_group_count=c)
    return y + b


def linear_scan(a, b):
    def combine(e1, e2):
        a1, b1 = e1
        a2, b2 = e2
        return a1 * a2, a2 * b1 + b2
    return lax.associative_scan(combine, (a, b), axis=1)[1]


def rg_lru(x, gx_w, gx_b, ga_w, ga_b, lru_L):
    bsz, seq, width = x.shape
    xf = x.astype(jnp.float32)
    xb = xf.reshape(bsz, seq, LRU_BLOCKS, width // LRU_BLOCKS)

    def block_diag(w, b):
        return jnp.einsum('bsnc,ncd->bsnd', xb, w.astype(jnp.float32)).reshape(bsz, seq, width) + b.astype(jnp.float32)

    i_gate = jax.nn.sigmoid(block_diag(gx_w, gx_b))
    r_gate = jax.nn.sigmoid(block_diag(ga_w, ga_b))
    log_a = -LRU_C * r_gate * jax.nn.softplus(-lru_L.astype(jnp.float32))
    a = jnp.exp(log_a)
    b = jnp.sqrt(-jnp.expm1(2.0 * log_a)) * (i_gate * xf)
    return linear_scan(a, b).astype(x.dtype)


def banded_attention(q, k, v, max_dist, sinks=None):
    bt, n, hk, g, e = q.shape
    nb = n // BLOCK
    qb = q.reshape(bt, nb, BLOCK, hk, g, e)

    def with_prev(t):
        tb = t.reshape(bt, nb, BLOCK, hk, e)
        prev = jnp.pad(tb[:, :-1], ((0, 0), (1, 0), (0, 0), (0, 0), (0, 0)))
        return jnp.concatenate([prev, tb], axis=2)

    kb, vb = with_prev(k), with_prev(v)
    s = jnp.einsum('bnqhge,bnkhe->bnhgqk', qb, kb).astype(jnp.float32) * (e ** -0.5)
    qi = jnp.arange(BLOCK)[:, None]
    kj = jnp.arange(2 * BLOCK)[None, :]
    dist = qi + BLOCK - kj
    key_pos = jnp.arange(nb)[:, None, None] * BLOCK + kj - BLOCK
    mask = (dist >= 0) & (dist <= max_dist) & (key_pos >= 0)
    s = jnp.where(mask[None, :, None, None], s, -jnp.inf)
    m = s.max(-1)
    if sinks is not None:
        sk = sinks.astype(jnp.float32)[None, None, :, :, None]
        m = jnp.maximum(m, sk)
    p = jnp.exp(s - m[..., None])
    denom = p.sum(-1)
    if sinks is not None:
        denom = denom + jnp.exp(sk - m)
    o = jnp.einsum('bnhgqk,bnkhe->bnqhge', p, vb.astype(jnp.float32))
    o = o / jnp.moveaxis(denom, -1, 2)[..., None]
    lse = jnp.moveaxis(m + jnp.log(denom), -1, 2).reshape(bt, n, hk, g)
    return o.reshape(bt, n, hk, g, e).astype(q.dtype), lse


def dilated_attention(q, k, v, window, dilation):
    bsz, seq = q.shape[:2]
    span = dilation * BLOCK
    seq_p = -(-seq // span) * span
    m = seq_p // dilation

    def to_sub(t):
        rest = t.shape[2:]
        t = jnp.pad(t, ((0, 0), (0, seq_p - seq)) + ((0, 0),) * len(rest))
        t = t.reshape((bsz, m, dilation) + rest)
        return jnp.moveaxis(t, 2, 1).reshape((bsz * dilation, m) + rest)

    def from_sub(t):
        rest = t.shape[2:]
        t = jnp.moveaxis(t.reshape((bsz, dilation, m) + rest), 1, 2)
        return t.reshape((bsz, seq_p) + rest)[:, :seq]

    o, lse = banded_attention(to_sub(q), to_sub(k), to_sub(v), window // dilation)
    return from_sub(o), from_sub(lse)


def s5_mixer(u, A_re, A_im, log_dt, B_re, B_im, C_re, C_im, D_skip, glu_w, glu_b):
    bsz, seq, _ = u.shape
    f32 = jnp.float32
    lam = lax.complex(A_re.astype(f32), A_im.astype(f32))
    dt = jnp.exp(log_dt.astype(f32))[:, None]
    lam_bar = jnp.exp(lam * dt)
    b_bar = ((lam_bar - 1.0) / lam)[:, :, None] * lax.complex(B_re.astype(f32), B_im.astype(f32))
    uf = u.astype(f32)
    ug = uf.reshape(bsz, seq, S5_GROUPS, S5_GROUP)
    bu = lax.complex(jnp.einsum('bsgc,gpc->bsgp', ug, b_bar.real),
                     jnp.einsum('bsgc,gpc->bsgp', ug, b_bar.imag))
    a = jnp.broadcast_to(lam_bar, (1, seq) + lam_bar.shape)
    state = linear_scan(a, bu)
    y = (jnp.einsum('bsgp,gcp->bsgc', state.real, C_re.astype(f32))
         - jnp.einsum('bsgp,gcp->bsgc', state.imag, C_im.astype(f32)))
    y = y.reshape(bsz, seq, S5_WIDTH) + D_skip.astype(f32) * uf
    z = jax.nn.gelu(y)
    return (z * jax.nn.sigmoid(z @ glu_w.astype(f32) + glu_b.astype(f32))).astype(u.dtype)


def conv_ffn(x, w_up, conv_w, conv_b, w_down):
    h = causal_depthwise_conv(x @ w_up, conv_w, conv_b)
    gate, val = h[..., :D_FF], h[..., D_FF:]
    return (jax.nn.silu(gate) * val) @ w_down


def even_layer(x, w_in, conv_w, conv_b, gx_w, gx_b, ga_w, ga_b, lru_L, sinks, w_out,
               ln1_g, ln1_b, ffn_up, ffn_conv_w, ffn_conv_b, ffn_down, ln2_g, ln2_b):
    bsz, seq, _ = x.shape
    h = x @ w_in
    o1 = LRU_WIDTH
    o2 = 2 * LRU_WIDTH
    o3 = o2 + SWA_Q
    o4 = o3 + SWA_KV
    xa, ga, q, k, v = h[..., :o1], h[..., o1:o2], h[..., o2:o3], h[..., o3:o4], h[..., o4:]
    ya = rg_lru(causal_depthwise_conv(xa, conv_w, conv_b), gx_w, gx_b, ga_w, ga_b, lru_L) * jax.nn.gelu(ga)
    g = SWA_HEADS // SWA_KV_HEADS
    ob, _ = banded_attention(q.reshape(bsz, seq, SWA_KV_HEADS, g, HEAD_DIM),
                             k.reshape(bsz, seq, SWA_KV_HEADS, HEAD_DIM),
                             v.reshape(bsz, seq, SWA_KV_HEADS, HEAD_DIM),
                             SWA_WINDOW - 1, sinks.reshape(SWA_KV_HEADS, g))
    mix = jnp.concatenate([ya, ob.reshape(bsz, seq, SWA_Q)], axis=-1) @ w_out
    x = layer_norm(ALPHA * x + mix, ln1_g, ln1_b)
    return layer_norm(ALPHA * x + conv_ffn(x, ffn_up, ffn_conv_w, ffn_conv_b, ffn_down), ln2_g, ln2_b)


def odd_layer(x, w_in, A_re, A_im, log_dt, B_re, B_im, C_re, C_im, D_skip, glu_w, glu_b, w_out,
              ln1_g, ln1_b, ffn_up, ffn_conv_w, ffn_conv_b, ffn_down, ln2_g, ln2_b):
    bsz, seq, _ = x.shape
    h = x @ w_in
    o1 = S5_WIDTH
    o2 = o1 + DIL_Q
    o3 = o2 + DIL_KV
    u, q, k, v = h[..., :o1], h[..., o1:o2], h[..., o2:o3], h[..., o3:]
    yc = s5_mixer(u, A_re, A_im, log_dt, B_re, B_im, C_re, C_im, D_skip, glu_w, glu_b)
    g = DIL_HEADS // DIL_KV_HEADS
    q = q.reshape(bsz, seq, len(DIL_CONFIGS), DIL_KV_HEADS, g, HEAD_DIM)
    k = k.reshape(bsz, seq, DIL_KV_HEADS, HEAD_DIM)
    v = v.reshape(bsz, seq, DIL_KV_HEADS, HEAD_DIM)
    outs, lses = [], []
    for r, (window, dilation) in enumerate(DIL_CONFIGS):
        o, l = dilated_attention(q[:, :, r], k, v, window, dilation)
        outs.append(o)
        lses.append(l)
    wts = jax.nn.softmax(jnp.stack(lses), axis=0)
    yd = (wts[..., None] * jnp.stack(outs).astype(jnp.float32)).sum(0)
    yd = yd.astype(x.dtype).reshape(bsz, seq, DIL_HEADS * HEAD_DIM)
    mix = jnp.concatenate([yc, yd], axis=-1) @ w_out
    x = layer_norm(ALPHA * x + mix, ln1_g, ln1_b)
    return layer_norm(ALPHA * x + conv_ffn(x, ffn_up, ffn_conv_w, ffn_conv_b, ffn_down), ln2_g, ln2_b)


def setup_inputs(seed: int = 0) -> dict:
    key = jax.random.key(seed)
    ks = iter(jax.random.split(key, 64))
    f32 = jnp.float32

    def nrm(shape, scale):
        return scale * jax.random.normal(next(ks), shape, f32)

    def gain(n):
        return 1.0 + nrm((n,), 0.01)

    def bias(n):
        return nrm((n,), 0.01)

    blk = LRU_WIDTH // LRU_BLOCKS
    a_c = jax.random.uniform(next(ks), (LRU_WIDTH,), f32, 0.9, 0.999)
    s_L = a_c ** (1.0 / LRU_C)
    log_dt = jax.random.uniform(next(ks), (S5_GROUPS,), f32, math.log(1e-3), math.log(1e-1))
    inp = {}
    inp['x'] = nrm((BATCH, SEQ, D_MODEL), 1.0)
    inp['l0_w_in'] = nrm((D_MODEL, EVEN_IN), D_MODEL ** -0.5)
    inp['l0_lru_conv_w'] = nrm((LRU_CONV, LRU_WIDTH), LRU_CONV ** -0.5)
    inp['l0_lru_conv_b'] = bias(LRU_WIDTH)
    inp['l0_lru_gx_w'] = nrm((LRU_BLOCKS, blk, blk), blk ** -0.5)
    inp['l0_lru_gx_b'] = bias(LRU_WIDTH)
    inp['l0_lru_ga_w'] = nrm((LRU_BLOCKS, blk, blk), blk ** -0.5)
    inp['l0_lru_ga_b'] = bias(LRU_WIDTH)
    inp['l0_lru_L'] = jnp.log(s_L) - jnp.log1p(-s_L)
    inp['l0_sinks'] = nrm((SWA_HEADS,), 0.5)
    inp['l0_w_out'] = nrm((EVEN_OUT, D_MODEL), BETA * EVEN_OUT ** -0.5)
    inp['l0_ln1_g'] = gain(D_MODEL)
    inp['l0_ln1_b'] = bias(D_MODEL)
    inp['l0_ffn_up'] = nrm((D_MODEL, 2 * D_FF), D_MODEL ** -0.5)
    inp['l0_ffn_conv_w'] = nrm((FFN_CONV, 2 * D_FF), FFN_CONV ** -0.5)
    inp['l0_ffn_conv_b'] = bias(2 * D_FF)
    inp['l0_ffn_down'] = nrm((D_FF, D_MODEL), BETA * D_FF ** -0.5)
    inp['l0_ln2_g'] = gain(D_MODEL)
    inp['l0_ln2_b'] = bias(D_MODEL)
    inp['l1_w_in'] = nrm((D_MODEL, ODD_IN), D_MODEL ** -0.5)
    inp['l1_s5_A_re'] = -0.5 + nrm((S5_GROUPS, S5_STATE), 0.01)
    inp['l1_s5_A_im'] = jnp.tile(math.pi * jnp.arange(S5_STATE, dtype=f32), (S5_GROUPS, 1))
    inp['l1_s5_log_dt'] = log_dt
    inp['l1_s5_B_re'] = nrm((S5_GROUPS, S5_STATE, S5_GROUP), (2.0 * S5_GROUP) ** -0.5)
    inp['l1_s5_B_im'] = nrm((S5_GROUPS, S5_STATE, S5_GROUP), (2.0 * S5_GROUP) ** -0.5)
    inp['l1_s5_C_re'] = nrm((S5_GROUPS, S5_GROUP, S5_STATE), (2.0 * S5_STATE) ** -0.5)
    inp['l1_s5_C_im'] = nrm((S5_GROUPS, S5_GROUP, S5_STATE), (2.0 * S5_STATE) ** -0.5)
    inp['l1_s5_D'] = nrm((S5_WIDTH,), 0.5)
    inp['l1_glu_w'] = nrm((S5_WIDTH, S5_WIDTH), S5_WIDTH ** -0.5)
    inp['l1_glu_b'] = bias(S5_WIDTH)
    inp['l1_w_out'] = nrm((ODD_OUT, D_MODEL), BETA * ODD_OUT ** -0.5)
    inp['l1_ln1_g'] = gain(D_MODEL)
    inp['l1_ln1_b'] = bias(D_MODEL)
    inp['l1_ffn_up'] = nrm((D_MODEL, 2 * D_FF), D_MODEL ** -0.5)
    inp['l1_ffn_conv_w'] = nrm((FFN_CONV, 2 * D_FF), FFN_CONV ** -0.5)
    inp['l1_ffn_conv_b'] = bias(2 * D_FF)
    inp['l1_ffn_down'] = nrm((D_FF, D_MODEL), BETA * D_FF ** -0.5)
    inp['l1_ln2_g'] = gain(D_MODEL)
    inp['l1_ln2_b'] = bias(D_MODEL)
    return inp


def reference(x,
              l0_w_in, l0_lru_conv_w, l0_lru_conv_b, l0_lru_gx_w, l0_lru_gx_b, l0_lru_ga_w, l0_lru_ga_b,
              l0_lru_L, l0_sinks, l0_w_out, l0_ln1_g, l0_ln1_b, l0_ffn_up, l0_ffn_conv_w, l0_ffn_conv_b,
              l0_ffn_down, l0_ln2_g, l0_ln2_b,
              l1_w_in, l1_s5_A_re, l1_s5_A_im, l1_s5_log_dt, l1_s5_B_re, l1_s5_B_im, l1_s5_C_re, l1_s5_C_im,
              l1_s5_D, l1_glu_w, l1_glu_b, l1_w_out, l1_ln1_g, l1_ln1_b, l1_ffn_up, l1_ffn_conv_w,
              l1_ffn_conv_b, l1_ffn_down, l1_ln2_g, l1_ln2_b):
    even_params = (l0_w_in, l0_lru_conv_w, l0_lru_conv_b, l0_lru_gx_w, l0_lru_gx_b, l0_lru_ga_w, l0_lru_ga_b,
                   l0_lru_L, l0_sinks, l0_w_out, l0_ln1_g, l0_ln1_b, l0_ffn_up, l0_ffn_conv_w, l0_ffn_conv_b,
                   l0_ffn_down, l0_ln2_g, l0_ln2_b)
    odd_params = (l1_w_in, l1_s5_A_re, l1_s5_A_im, l1_s5_log_dt, l1_s5_B_re, l1_s5_B_im, l1_s5_C_re, l1_s5_C_im,
                  l1_s5_D, l1_glu_w, l1_glu_b, l1_w_out, l1_ln1_g, l1_ln1_b, l1_ffn_up, l1_ffn_conv_w,
                  l1_ffn_conv_b, l1_ffn_down, l1_ln2_g, l1_ln2_b)
    for layer in range(DEPTH):
        if layer % 2 == 0:
            x = even_layer(x, *even_params)
        else:
            x = odd_layer(x, *odd_params)
    return x
```

```python
import functools
import math

import jax
import jax.numpy as jnp
from jax import lax
from jax.experimental import pallas as pl
from jax.experimental.pallas import tpu as pltpu

F32 = jnp.float32
BF16 = jnp.bfloat16

HEAD_DIM = 64
BLOCK = 128
LANES = 128
LRU_WIDTH = 1024
LRU_BLOCKS = 16
LRU_CONV = 4
LRU_C = 8.0
LRU_CHUNK = 256
SWA_HEADS = 16
SWA_KV_HEADS = 4
SWA_WINDOW = 128
S5_WIDTH = 768
S5_GROUP = 16
S5_GROUPS = S5_WIDTH // S5_GROUP
S5_STATE = 64
S5_GPB = LANES // S5_GROUP
S5_NBLK = S5_WIDTH // LANES
S5_HALF = S5_GPB * S5_STATE
DIL_CONFIGS = ((128, 1), (512, 4), (2048, 16))
DIL_HEADS = 8
DIL_KV_HEADS = 4
D_FF = 5632
FFN_CONV = 3
DEPTH = 2
ALPHA = (2 * DEPTH) ** 0.25
LN_EPS = 1e-5
MASK_BIAS = -1e30

V7X_VMEM_LIMIT_BYTES = 56 * 1024 * 1024


def _params(*sem):
    return pltpu.CompilerParams(dimension_semantics=sem, vmem_limit_bytes=V7X_VMEM_LIMIT_BYTES)


def _resident(shape):
    nd = len(shape)
    return pl.BlockSpec(shape, lambda *_: (0,) * nd, pipeline_mode=pl.Buffered(1))


def _sigmoid(x):
    return 1.0 / (1.0 + jnp.exp(-x))


def _gelu_tanh(x):
    return 0.5 * x * (1.0 + jnp.tanh(math.sqrt(2.0 / math.pi) * (x + 0.044715 * (x * x * x))))


def _layer_norm(y, g, b):
    mu = jnp.mean(y, axis=-1, keepdims=True)
    yc = y - mu
    var = jnp.mean(yc * yc, axis=-1, keepdims=True)
    return yc * lax.rsqrt(var + LN_EPS) * g + b


def _dot(a, b):
    return jnp.dot(a, b, preferred_element_type=F32)


def _inproj_kernel(x_ref, w_ref, *o_refs, chunk):
    xb = x_ref[...].astype(BF16)
    col = 0
    for o_ref in o_refs:
        width = o_ref.shape[1]
        for c0 in range(0, width, chunk):
            c1 = min(c0 + chunk, width)
            o_ref[:, c0:c1] = _dot(xb, w_ref[:, col + c0:col + c1]).astype(o_ref.dtype)
        col += width


def _inproj(x2d, w, outs, grid, x_map, out_views, out_maps, tm):
    d = x2d.shape[1]
    res = pl.pallas_call(
        functools.partial(_inproj_kernel, chunk=512),
        grid=grid,
        in_specs=[pl.BlockSpec((tm, d), x_map), _resident(w.shape)],
        out_specs=[pl.BlockSpec((tm, wd), m) for (wd, _), m in zip(outs, out_maps)],
        out_shape=[jax.ShapeDtypeStruct(v, dt) for (_, dt), v in zip(outs, out_views)],
        compiler_params=_params(*(("parallel",) * len(grid))),
        name="inproj",
    )(x2d, w)
    return res


def _lru_kernel(xa_ref, halo_ref, ga_ref, cw_ref, cb_ref, wx_ref, bx_ref, wa_ref, ba_ref, l_ref,
                y_ref, ext_s, a_s, h_s, carry_s, *, nb, tm):
    i = pl.program_id(1)
    halo_rows = (LRU_CONV - 1) * nb

    @pl.when(i == 0)
    def _():
        carry_s[...] = jnp.zeros_like(carry_s)

    halo = halo_ref[halo_ref.shape[0] - halo_rows:, :]
    ext_s[0:halo_rows, :] = jnp.where(i > 0, halo, 0.0)
    ext_s[halo_rows:, :] = xa_ref[...]
    xc = cb_ref[...]
    for k in range(LRU_CONV):
        xc = xc + cw_ref[k:k + 1, :] * ext_s[k * nb:k * nb + tm, :]

    xcb = xc.astype(BF16)
    i_gate = _sigmoid(_dot(xcb, wx_ref[0]) + bx_ref[...])
    r_gate = _sigmoid(_dot(xcb, wa_ref[0]) + ba_ref[...])
    neg_l = -l_ref[...]
    softplus = jnp.maximum(neg_l, 0.0) + jnp.log1p(jnp.exp(-jnp.abs(neg_l)))
    log_a = (-LRU_C * softplus) * r_gate
    a = jnp.exp(log_a)
    a_s[...] = a
    h_s[...] = jnp.sqrt(-jnp.tanh(log_a) * (a * a + 1.0)) * (i_gate * xc)

    def step(t, h):
        r0 = pl.multiple_of(t * nb, nb)
        h = a_s[pl.ds(r0, nb), :] * h + h_s[pl.ds(r0, nb), :]
        h_s[pl.ds(r0, nb), :] = h
        return h

    carry_s[...] = lax.fori_loop(0, tm // nb, step, carry_s[...], unroll=4)
    y_ref[...] = (h_s[...] * _gelu_tanh(ga_ref[...])).astype(y_ref.dtype)


def _rg_lru(h_a, conv_w, conv_b, wx4, bx, wa4, ba, lam, nb, tm, halo_blk):
    t = h_a.shape[0]
    nc = LRU_WIDTH // LRU_CHUNK
    c = LRU_CHUNK
    row = lambda cc, i: (0, cc)
    return pl.pallas_call(
        functools.partial(_lru_kernel, nb=nb, tm=tm),
        grid=(nc, t // tm),
        in_specs=[
            pl.BlockSpec((tm, c), lambda cc, i: (i, cc)),
            pl.BlockSpec((halo_blk, c), lambda cc, i: (jnp.maximum(i * (tm // halo_blk) - 1, 0), cc)),
            pl.BlockSpec((tm, c), lambda cc, i: (i, nc + cc)),
            pl.BlockSpec((LRU_CONV, c), row),
            pl.BlockSpec((1, c), row),
            pl.BlockSpec((1, c, c), lambda cc, i: (cc, 0, 0)),
            pl.BlockSpec((1, c), row),
            pl.BlockSpec((1, c, c), lambda cc, i: (cc, 0, 0)),
            pl.BlockSpec((1, c), row),
            pl.BlockSpec((1, c), row),
        ],
        out_specs=pl.BlockSpec((tm, c), lambda cc, i: (i, cc)),
        out_shape=jax.ShapeDtypeStruct((t, LRU_WIDTH), BF16),
        scratch_shapes=[
            pltpu.VMEM((tm + (LRU_CONV - 1) * nb, c), F32),
            pltpu.VMEM((tm, c), F32),
            pltpu.VMEM((tm, c), F32),
            pltpu.VMEM((nb, c), F32),
        ],
        compiler_params=_params("parallel", "arbitrary"),
        name="rg_lru",
    )(h_a, h_a, h_a, conv_w, conv_b, wx4, bx, wa4, ba, lam)


def _attn_kernel(*refs, g, pp, tq, max_dist, has_sinks, has_lse):
    refs = list(refs)
    sink_ref = refs.pop(0) if has_sinks else None
    q_ref, kc_ref, vc_ref, kp_ref, vp_ref, o_ref = refs[:6]
    lse_ref = refs[6] if has_lse else None
    i = pl.program_id(1)
    pbase = pl.program_id(2) * pp
    nblk = tq // BLOCK
    half = LANES // 2

    lane = lax.broadcasted_iota(jnp.int32, (1, LANES), 1)
    lo = lane < half
    qi = lax.broadcasted_iota(jnp.int32, (BLOCK, 2 * BLOCK), 0)
    kj = lax.broadcasted_iota(jnp.int32, (BLOCK, 2 * BLOCK), 1)
    dist = qi + BLOCK - kj
    band = (dist >= 0) & (dist <= max_dist)
    bias_any = jnp.where(band, 0.0, MASK_BIAS).astype(F32)
    bias_first = jnp.where(band & (kj >= BLOCK), 0.0, MASK_BIAS).astype(F32)

    for pl_i in range(pp):
        ks = slice(pl_i * LANES, (pl_i + 1) * LANES)
        kall = jnp.concatenate([kp_ref[:, ks], kc_ref[:, ks]], axis=0)
        vall = jnp.concatenate([vp_ref[:, ks], vc_ref[:, ks]], axis=0)
        kswp = jnp.concatenate([kall[:, half:], kall[:, :half]], axis=1)
        vswp = jnp.concatenate([vall[:, half:], vall[:, :half]], axis=1)
        zero = jnp.zeros_like(kall)
        k_a = (jnp.where(lo, kall, zero), jnp.where(lo, kswp, zero))
        k_b = (jnp.where(lo, zero, kswp), jnp.where(lo, zero, kall))
        v_a = (jnp.where(lo, vall, zero), jnp.where(lo, vswp, zero))
        v_b = (jnp.where(lo, zero, vswp), jnp.where(lo, zero, vall))
        for jb in range(nblk):
            bias = jnp.where(i * nblk + jb > 0, bias_any, bias_first)
            bias2 = jnp.concatenate([bias, bias], axis=1)
            rows = slice(jb * BLOCK, jb * BLOCK + 2 * BLOCK)
            qrows = slice(jb * BLOCK, (jb + 1) * BLOCK)
            for hh in range(2):
                kab = jnp.concatenate([k_a[hh][rows], k_b[hh][rows]], axis=0)
                vab = jnp.concatenate([v_a[hh][rows], v_b[hh][rows]], axis=0)
                for pr in range(g // 2):
                    c0 = ((pl_i * 2 + hh) * (g // 2) + pr) * LANES
                    qp = q_ref[qrows, c0:c0 + LANES]
                    s = lax.dot_general(qp, kab, (((1,), (1,)), ((), ())),
                                        preferred_element_type=F32) + bias2
                    s_a, s_b = s[:, :2 * BLOCK], s[:, 2 * BLOCK:]
                    m_a = jnp.max(s_a, axis=-1, keepdims=True)
                    m_b = jnp.max(s_b, axis=-1, keepdims=True)
                    if has_sinks:
                        head = ((pbase + pl_i) * 2 + hh) * g + pr * 2
                        sk_a = sink_ref[head]
                        sk_b = sink_ref[head + 1]
                        m_a = jnp.maximum(m_a, sk_a)
                        m_b = jnp.maximum(m_b, sk_b)
                    p_a = jnp.exp(s_a - m_a)
                    p_b = jnp.exp(s_b - m_b)
                    d_a = jnp.sum(p_a, axis=-1, keepdims=True)
                    d_b = jnp.sum(p_b, axis=-1, keepdims=True)
                    if has_sinks:
                        d_a = d_a + jnp.exp(sk_a - m_a)
                        d_b = d_b + jnp.exp(sk_b - m_b)
                    pcat = jnp.concatenate([p_a, p_b], axis=1).astype(BF16)
                    o = _dot(pcat, vab)
                    o_ref[qrows, c0:c0 + LANES] = (o / jnp.where(lo, d_a, d_b)).astype(o_ref.dtype)
                    if has_lse:
                        lse_ref[qrows, c0:c0 + LANES] = jnp.where(
                            lo, m_a + jnp.log(d_a), m_b + jnp.log(d_b))


def _banded_attention(q_arr, kv_arr, *, nseq, length, q_blk, q_off, g, n_kvh, max_dist, tq,
                      sinks=None, has_lse=False, out_dtype=BF16):
    qw = n_kvh * g * HEAD_DIM
    kw = n_kvh * HEAD_DIM
    pp = max(1, min(n_kvh // 2, 512 // (g * LANES)))
    nsteps = (n_kvh // 2) // pp
    qbw = pp * g * LANES
    kbw = pp * LANES
    rpb = tq // BLOCK
    qmap = lambda n, i, p: (i, (n * q_blk + q_off) * nsteps + p)
    kmap = lambda n, i, p: (i, n * 2 * nsteps + p)
    vmap = lambda n, i, p: (i, (n * 2 + 1) * nsteps + p)
    kpmap = lambda n, i, p: (jnp.maximum(i * rpb - 1, 0), n * 2 * nsteps + p)
    vpmap = lambda n, i, p: (jnp.maximum(i * rpb - 1, 0), (n * 2 + 1) * nsteps + p)
    omap = lambda n, i, p: (i, n * nsteps + p)
    in_specs = [
        pl.BlockSpec((tq, qbw), qmap),
        pl.BlockSpec((tq, kbw), kmap),
        pl.BlockSpec((tq, kbw), vmap),
        pl.BlockSpec((BLOCK, kbw), kpmap),
        pl.BlockSpec((BLOCK, kbw), vpmap),
    ]
    args = [q_arr, kv_arr, kv_arr, kv_arr, kv_arr]
    if sinks is not None:
        in_specs.insert(0, pl.BlockSpec(memory_space=pltpu.SMEM))
        args.insert(0, sinks)
    out_shape = [jax.ShapeDtypeStruct((length, nseq * qw), out_dtype)]
    out_specs = [pl.BlockSpec((tq, qbw), omap)]
    if has_lse:
        out_shape.append(jax.ShapeDtypeStruct((length, nseq * qw), F32))
        out_specs.append(pl.BlockSpec((tq, qbw), omap))
    return pl.pallas_call(
        functools.partial(_attn_kernel, g=g, pp=pp, tq=tq, max_dist=max_dist,
                          has_sinks=sinks is not None, has_lse=has_lse),
        grid=(nseq, length // tq, nsteps),
        in_specs=in_specs,
        out_specs=out_specs,
        out_shape=out_shape,
        compiler_params=_params("parallel", "parallel", "parallel"),
        name="banded_attn",
    )(*args)


def _s5_kernel(u_ref, bd_ref, lre_ref, lim_ref, cd_ref, d_ref, gw_ref, gb_ref, y_ref,
               sb_s, st_s, z_s, *, nb, tm):
    i = pl.program_id(0)
    j = pl.program_id(1)
    hw = S5_HALF

    @pl.when(i == 0)
    def _():
        st_s[j] = jnp.zeros(st_s.shape[1:], F32)

    u = u_ref[...]
    sb_s[...] = _dot(u.astype(BF16), bd_ref[...])
    lre = jnp.broadcast_to(lre_ref[...], (nb, hw))
    lim = jnp.broadcast_to(lim_ref[...], (nb, hw))

    def step(t, s):
        sre, sim = s
        r0 = pl.multiple_of(t * nb, nb)
        nre = lre * sre - lim * sim + sb_s[pl.ds(r0, nb), 0:hw]
        nim = lre * sim + lim * sre + sb_s[pl.ds(r0, nb), hw:2 * hw]
        sb_s[pl.ds(r0, nb), 0:hw] = nre
        sb_s[pl.ds(r0, nb), hw:2 * hw] = nim
        return nre, nim

    st = st_s[j]
    sre, sim = lax.fori_loop(0, tm // nb, step, (st[:, 0:hw], st[:, hw:2 * hw]), unroll=2)
    st_s[j] = jnp.concatenate([sre, sim], axis=1)
    y = _dot(sb_s[...].astype(BF16), cd_ref[...]) + d_ref[...] * u
    z_s[j] = _gelu_tanh(y)

    @pl.when(j == S5_NBLK - 1)
    def _():
        z = jnp.concatenate([z_s[k] for k in range(S5_NBLK)], axis=1)
        gate = _sigmoid(_dot(z.astype(BF16), gw_ref[...]) + gb_ref[...])
        y_ref[...] = (z * gate).astype(y_ref.dtype)


def _s5(u, bd, lre, lim, cd, dsk, glu_w, glu_b, nb, tm):
    t = u.shape[0]
    blk3 = lambda i, j: (j, 0, 0)
    return pl.pallas_call(
        functools.partial(_s5_kernel, nb=nb, tm=tm),
        grid=(t // tm, S5_NBLK),
        in_specs=[
            pl.BlockSpec((tm, LANES), lambda i, j: (i, j)),
            pl.BlockSpec((None, LANES, 2 * S5_HALF), blk3),
            pl.BlockSpec((None, 1, S5_HALF), blk3),
            pl.BlockSpec((None, 1, S5_HALF), blk3),
            pl.BlockSpec((None, 2 * S5_HALF, LANES), blk3),
            pl.BlockSpec((None, 1, LANES), blk3),
            _resident(glu_w.shape),
            _resident(glu_b.shape),
        ],
        out_specs=pl.BlockSpec((tm, S5_WIDTH), lambda i, j: (i, 0)),
        out_shape=jax.ShapeDtypeStruct((t, S5_WIDTH), BF16),
        scratch_shapes=[
            pltpu.VMEM((tm, 2 * S5_HALF), F32),
            pltpu.VMEM((S5_NBLK, nb, 2 * S5_HALF), F32),
            pltpu.VMEM((S5_NBLK, tm, LANES), F32),
        ],
        compiler_params=_params("arbitrary", "arbitrary"),
        name="s5",
    )(u, bd, lre, lim, cd, dsk, glu_w, glu_b)


def _outproj_even_kernel(ya_ref, ob_ref, x_ref, w_ref, g_ref, b_ref, o_ref, *, rows):
    k1 = ya_ref.shape[1]
    k2 = ob_ref.shape[1]
    for r0 in range(0, x_ref.shape[0], rows):
        rs = slice(r0, r0 + rows)
        mix = _dot(ya_ref[rs, :], w_ref[0:k1, :]) + _dot(ob_ref[rs, :], w_ref[k1:k1 + k2, :])
        o_ref[rs, :] = _layer_norm(ALPHA * x_ref[rs, :] + mix, g_ref[...], b_ref[...])


def _outproj_odd_kernel(yc_ref, o0_ref, o1_ref, o2_ref, l0_ref, l1_ref, l2_ref, x_ref, w_ref,
                        g_ref, b_ref, o_ref, *, rows):
    k1 = yc_ref.shape[1]
    k2 = o0_ref.shape[1]
    for r0 in range(0, x_ref.shape[0], rows):
        rs = slice(r0, r0 + rows)
        l0, l1, l2 = l0_ref[rs, :], l1_ref[rs, :], l2_ref[rs, :]
        m = jnp.maximum(jnp.maximum(l0, l1), l2)
        e0, e1, e2 = jnp.exp(l0 - m), jnp.exp(l1 - m), jnp.exp(l2 - m)
        yd = (e0 * o0_ref[rs, :] + e1 * o1_ref[rs, :] + e2 * o2_ref[rs, :]) / (e0 + e1 + e2)
        mix = _dot(yc_ref[rs, :], w_ref[0:k1, :]) + _dot(yd.astype(BF16), w_ref[k1:k1 + k2, :])
        o_ref[rs, :] = _layer_norm(ALPHA * x_ref[rs, :] + mix, g_ref[...], b_ref[...])


def _ffn_kernel(x_ref, wg_ref, wv_ref, cwg_ref, cwv_ref, cbg_ref, cbv_ref, wd_ref, g_ref, b_ref,
                o_ref, xb_s, hg_s, hv_s, cg_s, cv_s, acc_s, *, nb, tm, nj):
    i = pl.program_id(0)
    j = pl.program_id(1)
    halo = (FFN_CONV - 1) * nb

    @pl.when(j == 0)
    def _():
        xb_s[...] = x_ref[...].astype(BF16)
        acc_s[...] = jnp.zeros_like(acc_s)

    @pl.when(i == 0)
    def _():
        cg_s[j] = jnp.zeros(cg_s.shape[1:], F32)
        cv_s[j] = jnp.zeros(cv_s.shape[1:], F32)

    hg_s[0:halo, :] = cg_s[j]
    hv_s[0:halo, :] = cv_s[j]
    hg_s[halo:, :] = _dot(xb_s[...], wg_ref[...])
    hv_s[halo:, :] = _dot(xb_s[...], wv_ref[...])
    cg_s[j] = hg_s[tm:tm + halo, :]
    cv_s[j] = hv_s[tm:tm + halo, :]

    gate = cbg_ref[...]
    val = cbv_ref[...]
    for k in range(FFN_CONV):
        gate = gate + cwg_ref[k:k + 1, :] * hg_s[k * nb:k * nb + tm, :]
        val = val + cwv_ref[k:k + 1, :] * hv_s[k * nb:k * nb + tm, :]
    act = (gate * _sigmoid(gate) * val).astype(BF16)
    acc_s[...] += _dot(act, wd_ref[...])

    @pl.when(j == nj - 1)
    def _():
        o_ref[...] = _layer_norm(ALPHA * x_ref[...] + acc_s[...], g_ref[...], b_ref[...])


def _conv_ffn_ln(x, w_up, conv_w, conv_b, w_down, ln_g, ln_b, nb, tm, tf):
    t, d = x.shape
    nj = D_FF // tf
    halo = (FFN_CONV - 1) * nb
    return pl.pallas_call(
        functools.partial(_ffn_kernel, nb=nb, tm=tm, nj=nj),
        grid=(t // tm, nj),
        in_specs=[
            pl.BlockSpec((tm, d), lambda i, j: (i, 0)),
            pl.BlockSpec((d, tf), lambda i, j: (0, j)),
            pl.BlockSpec((d, tf), lambda i, j: (0, nj + j)),
            pl.BlockSpec((FFN_CONV, tf), lambda i, j: (0, j)),
            pl.BlockSpec((FFN_CONV, tf), lambda i, j: (0, nj + j)),
            pl.BlockSpec((1, tf), lambda i, j: (0, j)),
            pl.BlockSpec((1, tf), lambda i, j: (0, nj + j)),
            pl.BlockSpec((tf, d), lambda i, j: (j, 0)),
            _resident(ln_g.shape),
            _resident(ln_b.shape),
        ],
        out_specs=pl.BlockSpec((tm, d), lambda i, j: (i, 0)),
        out_shape=jax.ShapeDtypeStruct((t, d), F32),
        scratch_shapes=[
            pltpu.VMEM((tm, d), BF16),
            pltpu.VMEM((tm + halo, tf), F32),
            pltpu.VMEM((tm + halo, tf), F32),
            pltpu.VMEM((nj, halo, tf), F32),
            pltpu.VMEM((nj, halo, tf), F32),
            pltpu.VMEM((tm, d), F32),
        ],
        compiler_params=_params("arbitrary", "arbitrary"),
        name="conv_ffn_ln",
    )(x, w_up, w_up, conv_w, conv_w, conv_b, conv_b, w_down, ln_g, ln_b)


def _copy_kernel(x_ref, o_ref):
    o_ref[...] = x_ref[...]


def _to_batch_major(x_tm, bsz, seq, ts):
    d = x_tm.shape[1]
    nt = seq // ts
    return pl.pallas_call(
        _copy_kernel,
        grid=(bsz, nt),
        in_specs=[pl.BlockSpec((ts, d), lambda b, s: (s, b))],
        out_specs=pl.BlockSpec((ts, d), lambda b, s: (b * nt + s, 0)),
        out_shape=jax.ShapeDtypeStruct((bsz * seq, d), x_tm.dtype),
        compiler_params=_params("parallel", "parallel"),
        name="to_batch_major",
    )(x_tm.reshape(seq, bsz * d))


def _row(v):
    return v.astype(F32).reshape(1, -1)


def _block_diag(w, per):
    n, c, d = w.shape
    eye = jnp.eye(per, dtype=w.dtype)
    out = jnp.einsum('qncd,nm->qncmd', w.reshape(n // per, per, c, d), eye)
    return out.reshape(n // per, per * c, per * d)


def _s5_matrices(a_re, a_im, log_dt, b_re, b_im, c_re, c_im):
    a_re, a_im = a_re.astype(F32), a_im.astype(F32)
    dt = jnp.exp(log_dt.astype(F32))[:, None]
    mag = jnp.exp(a_re * dt)
    lre = mag * jnp.cos(a_im * dt)
    lim = mag * jnp.sin(a_im * dt)
    den = a_re * a_re + a_im * a_im
    cre = ((lre - 1.0) * a_re + lim * a_im) / den
    cim = (lim * a_re - (lre - 1.0) * a_im) / den
    bb_re = cre[:, :, None] * b_re - cim[:, :, None] * b_im
    bb_im = cre[:, :, None] * b_im + cim[:, :, None] * b_re
    eye = jnp.eye(S5_GPB, dtype=F32)
    shp = (S5_NBLK, S5_GPB, S5_STATE, S5_GROUP)
    bd = jnp.concatenate([
        jnp.einsum('jgpc,gh->jgchp', m.reshape(shp), eye).reshape(S5_NBLK, LANES, S5_HALF)
        for m in (bb_re, bb_im)], axis=2)
    shc = (S5_NBLK, S5_GPB, S5_GROUP, S5_STATE)
    cd = jnp.concatenate([
        jnp.einsum('jgcp,gh->jgphc', m.reshape(shc), eye).reshape(S5_NBLK, S5_HALF, LANES)
        for m in (c_re.astype(F32), -c_im.astype(F32))], axis=1)
    return (bd.astype(BF16), lre.reshape(S5_NBLK, 1, S5_HALF), lim.reshape(S5_NBLK, 1, S5_HALF),
            cd.astype(BF16))


def _even_layer(x_bm, bsz, seq, w_in, conv_w, conv_b, gx_w, gx_b, ga_w, ga_b, lru_l, sinks, w_out,
                ln1_g, ln1_b, ffn_up, ffn_conv_w, ffn_conv_b, ffn_down, ln2_g, ln2_b, *, ts, tm):
    d = x_bm.shape[1]
    t = bsz * seq
    nt = seq // ts
    qw = SWA_HEADS * HEAD_DIM
    kw = SWA_KV_HEADS * HEAD_DIM
    scale = jnp.concatenate([jnp.ones((2 * LRU_WIDTH,), F32),
                             jnp.full((qw,), HEAD_DIM ** -0.5, F32), jnp.ones((2 * kw,), F32)])
    w_in_b = (w_in.astype(F32) * scale).astype(BF16)
    outs = ((2 * LRU_WIDTH, F32), (qw, BF16), (2 * kw, BF16))
    h_a, h_q, h_kv = _inproj(
        x_bm, w_in_b, outs, (bsz, nt), lambda b, s: (b * nt + s, 0),
        [(seq, bsz * wd) for wd, _ in outs], [lambda b, s: (s, b)] * 3, ts)

    per = LRU_CHUNK // (LRU_WIDTH // LRU_BLOCKS)
    ya = _rg_lru(h_a.reshape(t, 2 * LRU_WIDTH), conv_w.astype(F32), _row(conv_b),
                 _block_diag(gx_w.astype(F32), per).astype(BF16), _row(gx_b),
                 _block_diag(ga_w.astype(F32), per).astype(BF16), _row(ga_b), _row(lru_l),
                 bsz, tm, 4 * bsz)

    ob, = _banded_attention(
        h_q, h_kv, nseq=bsz, length=seq, q_blk=1, q_off=0, g=SWA_HEADS // SWA_KV_HEADS,
        n_kvh=SWA_KV_HEADS, max_dist=SWA_WINDOW - 1, tq=2 * BLOCK, sinks=sinks.astype(F32))

    x1 = pl.pallas_call(
        functools.partial(_outproj_even_kernel, rows=256),
        grid=(bsz, nt),
        in_specs=[
            pl.BlockSpec((ts, LRU_WIDTH), lambda b, s: (s, b)),
            pl.BlockSpec((ts, qw), lambda b, s: (s, b)),
            pl.BlockSpec((ts, d), lambda b, s: (b * nt + s, 0)),
            _resident(w_out.shape), _resident((1, d)), _resident((1, d)),
        ],
        out_specs=pl.BlockSpec((ts, d), lambda b, s: (s, b)),
        out_shape=jax.ShapeDtypeStruct((seq, bsz * d), F32),
        compiler_params=_params("parallel", "parallel"),
        name="outproj_even",
    )(ya.reshape(seq, bsz * LRU_WIDTH), ob, x_bm, w_out.astype(BF16), _row(ln1_g), _row(ln1_b))

    return _conv_ffn_ln(x1.reshape(t, d), ffn_up.astype(BF16), ffn_conv_w.astype(F32),
                        _row(ffn_conv_b), ffn_down.astype(BF16), _row(ln2_g), _row(ln2_b),
                        bsz, tm, 512)


def _odd_layer(x, bsz, seq, w_in, a_re, a_im, log_dt, b_re, b_im, c_re, c_im, d_skip, glu_w, glu_b,
               w_out, ln1_g, ln1_b, ffn_up, ffn_conv_w, ffn_conv_b, ffn_down, ln2_g, ln2_b, *, tm):
    t, d = x.shape
    ncfg = len(DIL_CONFIGS)
    qw = DIL_HEADS * HEAD_DIM
    kw = DIL_KV_HEADS * HEAD_DIM
    scale = jnp.concatenate([jnp.ones((S5_WIDTH,), F32),
                             jnp.full((ncfg * qw,), HEAD_DIM ** -0.5, F32), jnp.ones((2 * kw,), F32)])
    w_in_b = (w_in.astype(F32) * scale).astype(BF16)
    outs = ((S5_WIDTH, F32), (ncfg * qw, BF16), (2 * kw, BF16))
    u, h_q, h_kv = _inproj(x, w_in_b, outs, (t // tm,), lambda i: (i, 0),
                           [(t, wd) for wd, _ in outs], [lambda i: (i, 0)] * 3, tm)

    bd, lre, lim, cd = _s5_matrices(a_re, a_im, log_dt, b_re, b_im, c_re, c_im)
    yc = _s5(u, bd, lre, lim, cd, d_skip.astype(F32).reshape(S5_NBLK, 1, LANES),
             glu_w.astype(BF16), _row(glu_b), bsz, tm)

    o_l = []
    for r, (window, dil) in enumerate(DIL_CONFIGS):
        length = seq // dil
        nseq = dil * bsz
        o, lse = _banded_attention(
            h_q.reshape(length, nseq * ncfg * qw), h_kv.reshape(length, nseq * 2 * kw),
            nseq=nseq, length=length, q_blk=ncfg, q_off=r, g=DIL_HEADS // DIL_KV_HEADS,
            n_kvh=DIL_KV_HEADS, max_dist=window // dil, tq=min(length, 4 * BLOCK),
            has_lse=True, out_dtype=F32)
        o_l.append((o.reshape(t, qw), lse.reshape(t, qw)))

    row = lambda i: (i, 0)
    x1 = pl.pallas_call(
        functools.partial(_outproj_odd_kernel, rows=256),
        grid=(t // tm,),
        in_specs=[pl.BlockSpec((tm, S5_WIDTH), row)]
        + [pl.BlockSpec((tm, qw), row)] * (2 * ncfg)
        + [pl.BlockSpec((tm, d), row), _resident(w_out.shape), _resident((1, d)), _resident((1, d))],
        out_specs=pl.BlockSpec((tm, d), row),
        out_shape=jax.ShapeDtypeStruct((t, d), F32),
        compiler_params=_params("parallel"),
        name="outproj_odd",
    )(yc, *[o for o, _ in o_l], *[l for _, l in o_l], x, w_out.astype(BF16), _row(ln1_g), _row(ln1_b))

    return _conv_ffn_ln(x1, ffn_up.astype(BF16), ffn_conv_w.astype(F32), _row(ffn_conv_b),
                        ffn_down.astype(BF16), _row(ln2_g), _row(ln2_b), bsz, tm, 512)


def kernel(x, l0_w_in, l0_lru_conv_w, l0_lru_conv_b, l0_lru_gx_w, l0_lru_gx_b, l0_lru_ga_w, l0_lru_ga_b, l0_lru_L, l0_sinks, l0_w_out, l0_ln1_g, l0_ln1_b, l0_ffn_up, l0_ffn_conv_w, l0_ffn_conv_b, l0_ffn_down, l0_ln2_g, l0_ln2_b, l1_w_in, l1_s5_A_re, l1_s5_A_im, l1_s5_log_dt, l1_s5_B_re, l1_s5_B_im, l1_s5_C_re, l1_s5_C_im, l1_s5_D, l1_glu_w, l1_glu_b, l1_w_out, l1_ln1_g, l1_ln1_b, l1_ffn_up, l1_ffn_conv_w, l1_ffn_conv_b, l1_ffn_down, l1_ln2_g, l1_ln2_b):
    bsz, seq, d = x.shape
    assert seq % (DIL_CONFIGS[-1][1] * BLOCK) == 0 and bsz % 8 == 0
    ts = min(seq, 512)
    tm = min(bsz * seq, 512)
    h = _even_layer(x.reshape(bsz * seq, d), bsz, seq, l0_w_in, l0_lru_conv_w, l0_lru_conv_b,
                    l0_lru_gx_w, l0_lru_gx_b, l0_lru_ga_w, l0_lru_ga_b, l0_lru_L, l0_sinks, l0_w_out,
                    l0_ln1_g, l0_ln1_b, l0_ffn_up, l0_ffn_conv_w, l0_ffn_conv_b, l0_ffn_down,
                    l0_ln2_g, l0_ln2_b, ts=ts, tm=tm)
    h = _odd_layer(h, bsz, seq, l1_w_in, l1_s5_A_re, l1_s5_A_im, l1_s5_log_dt, l1_s5_B_re, l1_s5_B_im,
                   l1_s5_C_re, l1_s5_C_im, l1_s5_D, l1_glu_w, l1_glu_b, l1_w_out, l1_ln1_g, l1_ln1_b,
                   l1_ffn_up, l1_ffn_conv_w, l1_ffn_conv_b, l1_ffn_down, l1_ln2_g, l1_ln2_b, tm=tm)
    return _to_batch_major(h, bsz, seq, ts).reshape(bsz, seq, d)
```

```python
import functools
import math

import jax
import jax.numpy as jnp
from jax import lax
from jax.experimental import pallas as pl
from jax.experimental.pallas import tpu as pltpu

F32 = jnp.float32
BF16 = jnp.bfloat16

HEAD_DIM = 64
BLOCK = 128
LANES = 128
LRU_WIDTH = 1024
LRU_BLOCKS = 16
LRU_CONV = 4
LRU_C = 8.0
LRU_CHUNK = 256
SWA_HEADS = 16
SWA_KV_HEADS = 4
SWA_WINDOW = 128
S5_WIDTH = 768
S5_GROUP = 16
S5_GROUPS = S5_WIDTH // S5_GROUP
S5_STATE = 64
S5_GPB = LANES // S5_GROUP
S5_NBLK = S5_WIDTH // LANES
S5_HALF = S5_GPB * S5_STATE
DIL_CONFIGS = ((128, 1), (512, 4), (2048, 16))
DIL_HEADS = 8
DIL_KV_HEADS = 4
D_FF = 5632
FFN_CONV = 3
DEPTH = 2
ALPHA = (2 * DEPTH) ** 0.25
LN_EPS = 1e-5
MASK_BIAS = -1e30
ATTN_SEQS = 16

V7X_VMEM_LIMIT_BYTES = 56 * 1024 * 1024


def _params(*sem):
    return pltpu.CompilerParams(dimension_semantics=sem, vmem_limit_bytes=V7X_VMEM_LIMIT_BYTES)


def _resident(shape):
    nd = len(shape)
    return pl.BlockSpec(shape, lambda *_: (0,) * nd, pipeline_mode=pl.Buffered(1))


def _sigmoid(x):
    return 1.0 / (1.0 + jnp.exp(-x))


def _gelu_tanh(x):
    return 0.5 * x * (1.0 + jnp.tanh(math.sqrt(2.0 / math.pi) * (x + 0.044715 * (x * x * x))))


def _layer_norm(y, g, b):
    mu = jnp.mean(y, axis=-1, keepdims=True)
    yc = y - mu
    var = jnp.mean(yc * yc, axis=-1, keepdims=True)
    return yc * lax.rsqrt(var + LN_EPS) * g + b


def _dot(a, b):
    return jnp.dot(a, b, preferred_element_type=F32)


def _inproj_kernel(x_ref, w_ref, *o_refs, chunk, batch_major_in):
    if batch_major_in:
        nb, ts, d = x_ref.shape
        x = pltpu.einshape("bsd->sbd", x_ref[...]).reshape(ts * nb, d)
        o_refs[-1][...] = x
        o_refs = o_refs[:-1]
    else:
        x = x_ref[...]
    xb = x.astype(BF16)
    col = 0
    for o_ref in o_refs:
        width = o_ref.shape[1]
        for c0 in range(0, width, chunk):
            c1 = min(c0 + chunk, width)
            o_ref[:, c0:c1] = _dot(xb, w_ref[:, col + c0:col + c1]).astype(o_ref.dtype)
        col += width


def _inproj(x, w, outs, tm, batch_major_in):
    if batch_major_in:
        nb, seq, d = x.shape
        t = nb * seq
        x_spec = pl.BlockSpec((nb, tm // nb, d), lambda i: (0, i, 0))
        outs = tuple(outs) + ((d, F32),)
    else:
        t, d = x.shape
        x_spec = pl.BlockSpec((tm, d), lambda i: (i, 0))
    return pl.pallas_call(
        functools.partial(_inproj_kernel, chunk=512, batch_major_in=batch_major_in),
        grid=(t // tm,),
        in_specs=[x_spec, _resident(w.shape)],
        out_specs=[pl.BlockSpec((tm, wd), lambda i: (i, 0)) for wd, _ in outs],
        out_shape=[jax.ShapeDtypeStruct((t, wd), dt) for wd, dt in outs],
        compiler_params=_params("parallel"),
        name="inproj",
    )(x, w)


def _lru_kernel(xa_ref, halo_ref, ga_ref, cw_ref, cb_ref, wx_ref, bx_ref, wa_ref, ba_ref, l_ref,
                y_ref, ext_s, a_s, h_s, carry_s, *, nb, tm):
    i = pl.program_id(1)
    halo_rows = (LRU_CONV - 1) * nb

    @pl.when(i == 0)
    def _():
        carry_s[...] = jnp.zeros_like(carry_s)

    halo = halo_ref[halo_ref.shape[0] - halo_rows:, :]
    ext_s[0:halo_rows, :] = jnp.where(i > 0, halo, 0.0)
    ext_s[halo_rows:, :] = xa_ref[...]
    xc = cb_ref[...]
    for k in range(LRU_CONV):
        xc = xc + cw_ref[k:k + 1, :] * ext_s[k * nb:k * nb + tm, :]

    xcb = xc.astype(BF16)
    i_gate = _sigmoid(_dot(xcb, wx_ref[0]) + bx_ref[...])
    r_gate = _sigmoid(_dot(xcb, wa_ref[0]) + ba_ref[...])
    neg_l = -l_ref[...]
    softplus = jnp.maximum(neg_l, 0.0) + jnp.log1p(jnp.exp(-jnp.abs(neg_l)))
    log_a = (-LRU_C * softplus) * r_gate
    a = jnp.exp(log_a)
    a_s[...] = a
    h_s[...] = jnp.sqrt(-jnp.tanh(log_a) * (a * a + 1.0)) * (i_gate * xc)

    def step(t, h):
        r0 = pl.multiple_of(t * nb, nb)
        h = a_s[pl.ds(r0, nb), :] * h + h_s[pl.ds(r0, nb), :]
        h_s[pl.ds(r0, nb), :] = h
        return h

    carry_s[...] = lax.fori_loop(0, tm // nb, step, carry_s[...], unroll=4)
    y_ref[...] = (h_s[...] * _gelu_tanh(ga_ref[...])).astype(y_ref.dtype)


def _rg_lru(h_a, conv_w, conv_b, wx4, bx, wa4, ba, lam, nb, tm, halo_blk):
    t = h_a.shape[0]
    nc = LRU_WIDTH // LRU_CHUNK
    c = LRU_CHUNK
    row = lambda cc, i: (0, cc)
    return pl.pallas_call(
        functools.partial(_lru_kernel, nb=nb, tm=tm),
        grid=(nc, t // tm),
        in_specs=[
            pl.BlockSpec((tm, c), lambda cc, i: (i, cc)),
            pl.BlockSpec((halo_blk, c), lambda cc, i: (jnp.maximum(i * (tm // halo_blk) - 1, 0), cc)),
            pl.BlockSpec((tm, c), lambda cc, i: (i, nc + cc)),
            pl.BlockSpec((LRU_CONV, c), row),
            pl.BlockSpec((1, c), row),
            pl.BlockSpec((1, c, c), lambda cc, i: (cc, 0, 0)),
            pl.BlockSpec((1, c), row),
            pl.BlockSpec((1, c, c), lambda cc, i: (cc, 0, 0)),
            pl.BlockSpec((1, c), row),
            pl.BlockSpec((1, c), row),
        ],
        out_specs=pl.BlockSpec((tm, c), lambda cc, i: (i, cc)),
        out_shape=jax.ShapeDtypeStruct((t, LRU_WIDTH), BF16),
        scratch_shapes=[
            pltpu.VMEM((tm + (LRU_CONV - 1) * nb, c), F32),
            pltpu.VMEM((tm, c), F32),
            pltpu.VMEM((tm, c), F32),
            pltpu.VMEM((nb, c), F32),
        ],
        compiler_params=_params("parallel", "arbitrary"),
        name="rg_lru",
    )(h_a, h_a, h_a, conv_w, conv_b, wx4, bx, wa4, ba, lam)


def _paired_head_order(n_kvh, g):
    order = []
    for p in range(n_kvh // 2):
        for j in range(g):
            order += [(2 * p) * g + j, (2 * p + 1) * g + j]
    return order


def _head_columns(order):
    return jnp.concatenate([jnp.arange(h * HEAD_DIM, (h + 1) * HEAD_DIM) for h in order])


def _attn_kernel(*refs, g, tq, max_dist, has_sinks, has_lse):
    refs = list(refs)
    sink_ref = refs.pop(0) if has_sinks else None
    q_ref, kc_ref, vc_ref, kp_ref, vp_ref, o_ref = refs[:6]
    refs = refs[6:]
    lse_ref = refs.pop(0) if has_lse else None
    q_s, k_s, v_s, o_s = refs[:4]
    l_s = refs[4] if has_lse else None
    i = pl.program_id(0)
    pair = pl.program_id(2)
    ns = q_ref.shape[1]
    nblk = tq // BLOCK
    half = LANES // 2
    nrb = 2 * g

    q_s[...] = pltpu.einshape("tnw->ntw", q_ref[...])
    k_s[:, 0:BLOCK, :] = pltpu.einshape("tnw->ntw", kp_ref[...])
    k_s[:, BLOCK:, :] = pltpu.einshape("tnw->ntw", kc_ref[...])
    v_s[:, 0:BLOCK, :] = pltpu.einshape("tnw->ntw", vp_ref[...])
    v_s[:, BLOCK:, :] = pltpu.einshape("tnw->ntw", vc_ref[...])

    lo = lax.broadcasted_iota(jnp.int32, (1, LANES), 1) < half
    qi = lax.broadcasted_iota(jnp.int32, (BLOCK, 2 * BLOCK), 0)
    kj = lax.broadcasted_iota(jnp.int32, (BLOCK, 2 * BLOCK), 1)
    dist = qi + BLOCK - kj
    band = (dist >= 0) & (dist <= max_dist)
    bias_any = jnp.where(band, 0.0, MASK_BIAS).astype(F32)
    bias_first = jnp.where(band & (kj >= BLOCK), 0.0, MASK_BIAS).astype(F32)
    ones = jnp.ones((2 * BLOCK, LANES), BF16)
    if has_sinks:
        sinks = [sink_ref[(2 * pair + (rb % 2)) * g + rb // 2] for rb in range(nrb)]

    def seq_body(n, carry):
        for jb in range(nblk):
            qrows = slice(jb * BLOCK, (jb + 1) * BLOCK)
            krows = slice(jb * BLOCK, (jb + 2) * BLOCK)
            parts = []
            for j in range(g):
                qj = q_s[n, qrows, j * LANES:(j + 1) * LANES]
                zero = jnp.zeros_like(qj)
                parts += [jnp.where(lo, qj, zero), jnp.where(lo, zero, qj)]
            lhs = jnp.concatenate(parts, axis=0)
            s = lax.dot_general(lhs, k_s[n, krows, :], (((1,), (1,)), ((), ())),
                                preferred_element_type=F32)
            bias = jnp.where(i * nblk + jb > 0, bias_any, bias_first)
            m, p = [], []
            for rb in range(nrb):
                s_rb = s[rb * BLOCK:(rb + 1) * BLOCK, :] + bias
                m_rb = jnp.max(s_rb, axis=-1, keepdims=True)
                if has_sinks:
                    m_rb = jnp.maximum(m_rb, sinks[rb])
                m.append(m_rb)
                p.append(jnp.exp(s_rb - m_rb).astype(BF16))
            vones = jnp.concatenate([v_s[n, krows, :], ones], axis=1)
            od = _dot(jnp.concatenate(p, axis=0), vones)
            den = []
            for rb in range(nrb):
                d_rb = od[rb * BLOCK:(rb + 1) * BLOCK, LANES:2 * LANES]
                if has_sinks:
                    d_rb = d_rb + jnp.exp(sinks[rb] - m[rb])
                den.append(d_rb)
            for j in range(g):
                ra, rb = 2 * j, 2 * j + 1
                val = jnp.where(lo, od[ra * BLOCK:(ra + 1) * BLOCK, 0:LANES],
                                od[rb * BLOCK:(rb + 1) * BLOCK, 0:LANES])
                dsel = jnp.where(lo, den[ra], den[rb])
                o_s[n, qrows, j * LANES:(j + 1) * LANES] = (val / dsel).astype(o_s.dtype)
                if has_lse:
                    l_s[n, qrows, j * LANES:(j + 1) * LANES] = (
                        jnp.where(lo, m[ra], m[rb]) + jnp.log(dsel))
        return carry

    lax.fori_loop(0, ns, seq_body, 0)
    o_ref[...] = pltpu.einshape("ntw->tnw", o_s[...])
    if has_lse:
        lse_ref[...] = pltpu.einshape("ntw->tnw", l_s[...])


def _banded_attention(q3, kv3, *, q_blk0, g, n_kvh, max_dist, tq, sinks=None, has_lse=False,
                      out_dtype=BF16):
    length, nseq, _ = q3.shape
    ns = min(ATTN_SEQS, nseq)
    npair = n_kvh // 2
    qbw = g * LANES
    rpb = tq // BLOCK
    cur = lambda off: (lambda i, n, p: (i, n, off + p))
    prev = lambda off: (lambda i, n, p: (jnp.maximum(i * rpb - 1, 0), n, off + p))
    in_specs = [
        pl.BlockSpec((tq, ns, qbw), cur(q_blk0)),
        pl.BlockSpec((tq, ns, LANES), cur(0)),
        pl.BlockSpec((tq, ns, LANES), cur(npair)),
        pl.BlockSpec((BLOCK, ns, LANES), prev(0)),
        pl.BlockSpec((BLOCK, ns, LANES), prev(npair)),
    ]
    args = [q3, kv3, kv3, kv3, kv3]
    if sinks is not None:
        in_specs.insert(0, pl.BlockSpec(memory_space=pltpu.SMEM))
        args.insert(0, sinks)
    ow = n_kvh * g * HEAD_DIM
    out_shape = [jax.ShapeDtypeStruct((length, nseq, ow), out_dtype)]
    out_specs = [pl.BlockSpec((tq, ns, qbw), cur(0))]
    scratch = [
        pltpu.VMEM((ns, tq, qbw), BF16),
        pltpu.VMEM((ns, tq + BLOCK, LANES), BF16),
        pltpu.VMEM((ns, tq + BLOCK, LANES), BF16),
        pltpu.VMEM((ns, tq, qbw), out_dtype),
    ]
    if has_lse:
        out_shape.append(jax.ShapeDtypeStruct((length, nseq, ow), F32))
        out_specs.append(pl.BlockSpec((tq, ns, qbw), cur(0)))
        scratch.append(pltpu.VMEM((ns, tq, qbw), F32))
    return pl.pallas_call(
        functools.partial(_attn_kernel, g=g, tq=tq, max_dist=max_dist,
                          has_sinks=sinks is not None, has_lse=has_lse),
        grid=(length // tq, nseq // ns, npair),
        in_specs=in_specs,
        out_specs=out_specs,
        out_shape=out_shape,
        scratch_shapes=scratch,
        compiler_params=_params("parallel", "parallel", "parallel"),
        name="banded_attn",
    )(*args)


def _s5_kernel(u_ref, bd_ref, lre_ref, lim_ref, cd_ref, d_ref, gw_ref, gb_ref, y_ref,
               sb_s, st_s, z_s, *, nb, tm):
    i = pl.program_id(0)
    j = pl.program_id(1)
    hw = S5_HALF

    @pl.when(i == 0)
    def _():
        st_s[j] = jnp.zeros(st_s.shape[1:], F32)

    u = u_ref[...]
    sb_s[...] = _dot(u.astype(BF16), bd_ref[...])
    lre = jnp.broadcast_to(lre_ref[...], (nb, hw))
    lim = jnp.broadcast_to(lim_ref[...], (nb, hw))

    def step(t, s):
        sre, sim = s
        r0 = pl.multiple_of(t * nb, nb)
        nre = lre * sre - lim * sim + sb_s[pl.ds(r0, nb), 0:hw]
        nim = lre * sim + lim * sre + sb_s[pl.ds(r0, nb), hw:2 * hw]
        sb_s[pl.ds(r0, nb), 0:hw] = nre
        sb_s[pl.ds(r0, nb), hw:2 * hw] = nim
        return nre, nim

    st = st_s[j]
    sre, sim = lax.fori_loop(0, tm // nb, step, (st[:, 0:hw], st[:, hw:2 * hw]), unroll=2)
    st_s[j] = jnp.concatenate([sre, sim], axis=1)
    y = _dot(sb_s[...].astype(BF16), cd_ref[...]) + d_ref[...] * u
    z_s[j] = _gelu_tanh(y)

    @pl.when(j == S5_NBLK - 1)
    def _():
        z = jnp.concatenate([z_s[k] for k in range(S5_NBLK)], axis=1)
        gate = _sigmoid(_dot(z.astype(BF16), gw_ref[...]) + gb_ref[...])
        y_ref[...] = (z * gate).astype(y_ref.dtype)


def _s5(u, bd, lre, lim, cd, dsk, glu_w, glu_b, nb, tm):
    t = u.shape[0]
    blk3 = lambda i, j: (j, 0, 0)
    return pl.pallas_call(
        functools.partial(_s5_kernel, nb=nb, tm=tm),
        grid=(t // tm, S5_NBLK),
        in_specs=[
            pl.BlockSpec((tm, LANES), lambda i, j: (i, j)),
            pl.BlockSpec((None, LANES, 2 * S5_HALF), blk3),
            pl.BlockSpec((None, 1, S5_HALF), blk3),
            pl.BlockSpec((None, 1, S5_HALF), blk3),
            pl.BlockSpec((None, 2 * S5_HALF, LANES), blk3),
            pl.BlockSpec((None, 1, LANES), blk3),
            _resident(glu_w.shape),
            _resident(glu_b.shape),
        ],
        out_specs=pl.BlockSpec((tm, S5_WIDTH), lambda i, j: (i, 0)),
        out_shape=jax.ShapeDtypeStruct((t, S5_WIDTH), BF16),
        scratch_shapes=[
            pltpu.VMEM((tm, 2 * S5_HALF), F32),
            pltpu.VMEM((S5_NBLK, nb, 2 * S5_HALF), F32),
            pltpu.VMEM((S5_NBLK, tm, LANES), F32),
        ],
        compiler_params=_params("arbitrary", "arbitrary"),
        name="s5",
    )(u, bd, lre, lim, cd, dsk, glu_w, glu_b)


def _outproj_even_kernel(ya_ref, ob_ref, x_ref, w_ref, g_ref, b_ref, o_ref, *, rows):
    k1 = ya_ref.shape[1]
    k2 = ob_ref.shape[1]
    for r0 in range(0, x_ref.shape[0], rows):
        rs = slice(r0, r0 + rows)
        mix = _dot(ya_ref[rs, :], w_ref[0:k1, :]) + _dot(ob_ref[rs, :], w_ref[k1:k1 + k2, :])
        o_ref[rs, :] = _layer_norm(ALPHA * x_ref[rs, :] + mix, g_ref[...], b_ref[...])


def _outproj_odd_kernel(yc_ref, o0_ref, o1_ref, o2_ref, l0_ref, l1_ref, l2_ref, x_ref, w_ref,
                        g_ref, b_ref, o_ref, *, rows):
    k1 = yc_ref.shape[1]
    k2 = o0_ref.shape[1]
    for r0 in range(0, x_ref.shape[0], rows):
        rs = slice(r0, r0 + rows)
        l0, l1, l2 = l0_ref[rs, :], l1_ref[rs, :], l2_ref[rs, :]
        m = jnp.maximum(jnp.maximum(l0, l1), l2)
        e0, e1, e2 = jnp.exp(l0 - m), jnp.exp(l1 - m), jnp.exp(l2 - m)
        yd = (e0 * o0_ref[rs, :] + e1 * o1_ref[rs, :] + e2 * o2_ref[rs, :]) / (e0 + e1 + e2)
        mix = _dot(yc_ref[rs, :], w_ref[0:k1, :]) + _dot(yd.astype(BF16), w_ref[k1:k1 + k2, :])
        o_ref[rs, :] = _layer_norm(ALPHA * x_ref[rs, :] + mix, g_ref[...], b_ref[...])


def _outproj_ln(kernel_fn, acts, x, w, ln_g, ln_b, tm, name):
    t, d = x.shape
    row = lambda i: (i, 0)
    return pl.pallas_call(
        functools.partial(kernel_fn, rows=256),
        grid=(t // tm,),
        in_specs=[pl.BlockSpec((tm, a.shape[1]), row) for a in acts]
        + [pl.BlockSpec((tm, d), row), _resident(w.shape), _resident((1, d)), _resident((1, d))],
        out_specs=pl.BlockSpec((tm, d), row),
        out_shape=jax.ShapeDtypeStruct((t, d), F32),
        compiler_params=_params("parallel"),
        name=name,
    )(*acts, x, w, ln_g, ln_b)


def _ffn_kernel(x_ref, wg_ref, wv_ref, cwg_ref, cwv_ref, cbg_ref, cbv_ref, wd_ref, g_ref, b_ref,
                o_ref, xb_s, hg_s, hv_s, cg_s, cv_s, acc_s, *, nb, tm, nj, batch_major_out):
    i = pl.program_id(0)
    j = pl.program_id(1)
    halo = (FFN_CONV - 1) * nb

    @pl.when(j == 0)
    def _():
        xb_s[...] = x_ref[...].astype(BF16)
        acc_s[...] = jnp.zeros_like(acc_s)

    @pl.when(i == 0)
    def _():
        cg_s[j] = jnp.zeros(cg_s.shape[1:], F32)
        cv_s[j] = jnp.zeros(cv_s.shape[1:], F32)

    hg_s[0:halo, :] = cg_s[j]
    hv_s[0:halo, :] = cv_s[j]
    hg_s[halo:, :] = _dot(xb_s[...], wg_ref[...])
    hv_s[halo:, :] = _dot(xb_s[...], wv_ref[...])
    cg_s[j] = hg_s[tm:tm + halo, :]
    cv_s[j] = hv_s[tm:tm + halo, :]

    gate = cbg_ref[...]
    val = cbv_ref[...]
    for k in range(FFN_CONV):
        gate = gate + cwg_ref[k:k + 1, :] * hg_s[k * nb:k * nb + tm, :]
        val = val + cwv_ref[k:k + 1, :] * hv_s[k * nb:k * nb + tm, :]
    act = (gate * _sigmoid(gate) * val).astype(BF16)
    acc_s[...] += _dot(act, wd_ref[...])

    @pl.when(j == nj - 1)
    def _():
        y = _layer_norm(ALPHA * x_ref[...] + acc_s[...], g_ref[...], b_ref[...])
        if batch_major_out:
            y = pltpu.einshape("sbd->bsd", y.reshape(tm // nb, nb, y.shape[1]))
        o_ref[...] = y


def _conv_ffn_ln(x, w_up, conv_w, conv_b, w_down, ln_g, ln_b, nb, tm, tf, batch_major_out):
    t, d = x.shape
    nj = D_FF // tf
    halo = (FFN_CONV - 1) * nb
    if batch_major_out:
        out_spec = pl.BlockSpec((nb, tm // nb, d), lambda i, j: (0, i, 0))
        out_shape = jax.ShapeDtypeStruct((nb, t // nb, d), F32)
    else:
        out_spec = pl.BlockSpec((tm, d), lambda i, j: (i, 0))
        out_shape = jax.ShapeDtypeStruct((t, d), F32)
    return pl.pallas_call(
        functools.partial(_ffn_kernel, nb=nb, tm=tm, nj=nj, batch_major_out=batch_major_out),
        grid=(t // tm, nj),
        in_specs=[
            pl.BlockSpec((tm, d), lambda i, j: (i, 0)),
            pl.BlockSpec((d, tf), lambda i, j: (0, j)),
            pl.BlockSpec((d, tf), lambda i, j: (0, nj + j)),
            pl.BlockSpec((FFN_CONV, tf), lambda i, j: (0, j)),
            pl.BlockSpec((FFN_CONV, tf), lambda i, j: (0, nj + j)),
            pl.BlockSpec((1, tf), lambda i, j: (0, j)),
            pl.BlockSpec((1, tf), lambda i, j: (0, nj + j)),
            pl.BlockSpec((tf, d), lambda i, j: (j, 0)),
            _resident(ln_g.shape),
            _resident(ln_b.shape),
        ],
        out_specs=out_spec,
        out_shape=out_shape,
        scratch_shapes=[
            pltpu.VMEM((tm, d), BF16),
            pltpu.VMEM((tm + halo, tf), F32),
            pltpu.VMEM((tm + halo, tf), F32),
            pltpu.VMEM((nj, halo, tf), F32),
            pltpu.VMEM((nj, halo, tf), F32),
            pltpu.VMEM((tm, d), F32),
        ],
        compiler_params=_params("arbitrary", "arbitrary"),
        name="conv_ffn_ln",
    )(x, w_up, w_up, conv_w, conv_w, conv_b, conv_b, w_down, ln_g, ln_b)


def _row(v):
    return v.astype(F32).reshape(1, -1)


def _block_diag(w, per):
    n, c, d = w.shape
    eye = jnp.eye(per, dtype=w.dtype)
    out = jnp.einsum('qncd,nm->qncmd', w.reshape(n // per, per, c, d), eye)
    return out.reshape(n // per, per * c, per * d)


def _s5_matrices(a_re, a_im, log_dt, b_re, b_im, c_re, c_im):
    a_re, a_im = a_re.astype(F32), a_im.astype(F32)
    dt = jnp.exp(log_dt.astype(F32))[:, None]
    mag = jnp.exp(a_re * dt)
    lre = mag * jnp.cos(a_im * dt)
    lim = mag * jnp.sin(a_im * dt)
    den = a_re * a_re + a_im * a_im
    cre = ((lre - 1.0) * a_re + lim * a_im) / den
    cim = (lim * a_re - (lre - 1.0) * a_im) / den
    bb_re = cre[:, :, None] * b_re - cim[:, :, None] * b_im
    bb_im = cre[:, :, None] * b_im + cim[:, :, None] * b_re
    eye = jnp.eye(S5_GPB, dtype=F32)
    shp = (S5_NBLK, S5_GPB, S5_STATE, S5_GROUP)
    bd = jnp.concatenate([
        jnp.einsum('jgpc,gh->jgchp', m.reshape(shp), eye).reshape(S5_NBLK, LANES, S5_HALF)
        for m in (bb_re, bb_im)], axis=2)
    shc = (S5_NBLK, S5_GPB, S5_GROUP, S5_STATE)
    cd = jnp.concatenate([
        jnp.einsum('jgcp,gh->jgphc', m.reshape(shc), eye).reshape(S5_NBLK, S5_HALF, LANES)
        for m in (c_re.astype(F32), -c_im.astype(F32))], axis=1)
    return (bd.astype(BF16), lre.reshape(S5_NBLK, 1, S5_HALF), lim.reshape(S5_NBLK, 1, S5_HALF),
            cd.astype(BF16))


def _even_layer(x, w_in, conv_w, conv_b, gx_w, gx_b, ga_w, ga_b, lru_l, sinks, w_out,
                ln1_g, ln1_b, ffn_up, ffn_conv_w, ffn_conv_b, ffn_down, ln2_g, ln2_b, *, tm):
    bsz, seq, d = x.shape
    t = bsz * seq
    g = SWA_HEADS // SWA_KV_HEADS
    qw = SWA_HEADS * HEAD_DIM
    kw = SWA_KV_HEADS * HEAD_DIM
    qcols = _head_columns(_paired_head_order(SWA_KV_HEADS, g))
    wf = w_in.astype(F32)
    o_q = 2 * LRU_WIDTH
    w_in_b = jnp.concatenate(
        [wf[:, :o_q], wf[:, o_q:o_q + qw][:, qcols] * HEAD_DIM ** -0.5, wf[:, o_q + qw:]],
        axis=1).astype(BF16)
    outs = ((2 * LRU_WIDTH, F32), (qw, BF16), (2 * kw, BF16))
    h_a, h_q, h_kv, x_tm = _inproj(x, w_in_b, outs, tm, True)

    per = LRU_CHUNK // (LRU_WIDTH // LRU_BLOCKS)
    ya = _rg_lru(h_a, conv_w.astype(F32), _row(conv_b),
                 _block_diag(gx_w.astype(F32), per).astype(BF16), _row(gx_b),
                 _block_diag(ga_w.astype(F32), per).astype(BF16), _row(ga_b), _row(lru_l),
                 bsz, tm, 4 * bsz)

    ob, = _banded_attention(
        h_q.reshape(seq, bsz, qw), h_kv.reshape(seq, bsz, 2 * kw), q_blk0=0, g=g,
        n_kvh=SWA_KV_HEADS, max_dist=SWA_WINDOW - 1, tq=2 * BLOCK, sinks=sinks.astype(F32))

    w_out_b = jnp.concatenate([w_out[:LRU_WIDTH], w_out[LRU_WIDTH:][qcols]], axis=0).astype(BF16)
    x1 = _outproj_ln(_outproj_even_kernel, [ya, ob.reshape(t, qw)], x_tm, w_out_b,
                     _row(ln1_g), _row(ln1_b), tm, "outproj_even")
    return _conv_ffn_ln(x1, ffn_up.astype(BF16), ffn_conv_w.astype(F32), _row(ffn_conv_b),
                        ffn_down.astype(BF16), _row(ln2_g), _row(ln2_b), bsz, tm, 512, False)


def _odd_layer(x, bsz, w_in, a_re, a_im, log_dt, b_re, b_im, c_re, c_im, d_skip, glu_w, glu_b,
               w_out, ln1_g, ln1_b, ffn_up, ffn_conv_w, ffn_conv_b, ffn_down, ln2_g, ln2_b, *, tm):
    t, d = x.shape
    seq = t // bsz
    ncfg = len(DIL_CONFIGS)
    g = DIL_HEADS // DIL_KV_HEADS
    qw = DIL_HEADS * HEAD_DIM
    kw = DIL_KV_HEADS * HEAD_DIM
    qcols = _head_columns(_paired_head_order(DIL_KV_HEADS, g))
    wf = w_in.astype(F32)
    o_q = S5_WIDTH
    w_q = jnp.concatenate([wf[:, o_q + r * qw:o_q + (r + 1) * qw][:, qcols] for r in range(ncfg)],
                          axis=1) * HEAD_DIM ** -0.5
    w_in_b = jnp.concatenate([wf[:, :o_q], w_q, wf[:, o_q + ncfg * qw:]], axis=1).astype(BF16)
    outs = ((S5_WIDTH, F32), (ncfg * qw, BF16), (2 * kw, BF16))
    u, h_q, h_kv = _inproj(x, w_in_b, outs, tm, False)

    bd, lre, lim, cd = _s5_matrices(a_re, a_im, log_dt, b_re, b_im, c_re, c_im)
    yc = _s5(u, bd, lre, lim, cd, d_skip.astype(F32).reshape(S5_NBLK, 1, LANES),
             glu_w.astype(BF16), _row(glu_b), bsz, tm)

    o_l = []
    for r, (window, dil) in enumerate(DIL_CONFIGS):
        length = seq // dil
        nseq = dil * bsz
        o, lse = _banded_attention(
            h_q.reshape(length, nseq, ncfg * qw), h_kv.reshape(length, nseq, 2 * kw),
            q_blk0=r * (DIL_KV_HEADS // 2), g=g, n_kvh=DIL_KV_HEADS, max_dist=window // dil,
            tq=min(length, 2 * BLOCK), has_lse=True, out_dtype=F32)
        o_l.append((o.reshape(t, qw), lse.reshape(t, qw)))

    w_out_b = jnp.concatenate([w_out[:S5_WIDTH], w_out[S5_WIDTH:][qcols]], axis=0).astype(BF16)
    x1 = _outproj_ln(_outproj_odd_kernel, [yc] + [o for o, _ in o_l] + [l for _, l in o_l], x,
                     w_out_b, _row(ln1_g), _row(ln1_b), tm, "outproj_odd")
    return _conv_ffn_ln(x1, ffn_up.astype(BF16), ffn_conv_w.astype(F32), _row(ffn_conv_b),
                        ffn_down.astype(BF16), _row(ln2_g), _row(ln2_b), bsz, tm, 512, True)


def kernel(x, l0_w_in, l0_lru_conv_w, l0_lru_conv_b, l0_lru_gx_w, l0_lru_gx_b, l0_lru_ga_w, l0_lru_ga_b, l0_lru_L, l0_sinks, l0_w_out, l0_ln1_g, l0_ln1_b, l0_ffn_up, l0_ffn_conv_w, l0_ffn_conv_b, l0_ffn_down, l0_ln2_g, l0_ln2_b, l1_w_in, l1_s5_A_re, l1_s5_A_im, l1_s5_log_dt, l1_s5_B_re, l1_s5_B_im, l1_s5_C_re, l1_s5_C_im, l1_s5_D, l1_glu_w, l1_glu_b, l1_w_out, l1_ln1_g, l1_ln1_b, l1_ffn_up, l1_ffn_conv_w, l1_ffn_conv_b, l1_ffn_down, l1_ln2_g, l1_ln2_b):
    bsz, seq, d = x.shape
    assert seq % (DIL_CONFIGS[-1][1] * BLOCK) == 0 and bsz % 8 == 0
    tm = 32 * bsz
    h = _even_layer(x, l0_w_in, l0_lru_conv_w, l0_lru_conv_b, l0_lru_gx_w, l0_lru_gx_b, l0_lru_ga_w,
                    l0_lru_ga_b, l0_lru_L, l0_sinks, l0_w_out, l0_ln1_g, l0_ln1_b, l0_ffn_up,
                    l0_ffn_conv_w, l0_ffn_conv_b, l0_ffn_down, l0_ln2_g, l0_ln2_b, tm=tm)
    return _odd_layer(h, bsz, l1_w_in, l1_s5_A_re, l1_s5_A_im, l1_s5_log_dt, l1_s5_B_re, l1_s5_B_im,
                      l1_s5_C_re, l1_s5_C_im, l1_s5_D, l1_glu_w, l1_glu_b, l1_w_out, l1_ln1_g, l1_ln1_b,
                      l1_ffn_up, l1_ffn_conv_w, l1_ffn_conv_b, l1_ffn_down, l1_ln2_g, l1_ln2_b, tm=tm)
```

```python
import functools
import math

import jax
import jax.numpy as jnp
from jax import lax
from jax.experimental import pallas as pl
from jax.experimental.pallas import tpu as pltpu

F32 = jnp.float32
BF16 = jnp.bfloat16

HEAD_DIM = 64
BLOCK = 128
LANES = 128
LRU_WIDTH = 1024
LRU_BLOCKS = 16
LRU_CONV = 4
LRU_C = 8.0
LRU_CHUNK = 256
SWA_HEADS = 16
SWA_KV_HEADS = 4
SWA_WINDOW = 128
S5_WIDTH = 768
S5_GROUP = 16
S5_GROUPS = S5_WIDTH // S5_GROUP
S5_STATE = 64
S5_GPB = LANES // S5_GROUP
S5_NBLK = S5_WIDTH // LANES
S5_HALF = S5_GPB * S5_STATE
DIL_CONFIGS = ((128, 1), (512, 4), (2048, 16))
DIL_HEADS = 8
DIL_KV_HEADS = 4
D_FF = 5632
FFN_CONV = 3
DEPTH = 2
ALPHA = (2 * DEPTH) ** 0.25
LN_EPS = 1e-5
MASK_BIAS = -1e30
ATTN_SEQS = 16

V7X_VMEM_LIMIT_BYTES = 56 * 1024 * 1024


def _params(*sem):
    return pltpu.CompilerParams(dimension_semantics=sem, vmem_limit_bytes=V7X_VMEM_LIMIT_BYTES)


def _resident(shape):
    nd = len(shape)
    return pl.BlockSpec(shape, lambda *_: (0,) * nd, pipeline_mode=pl.Buffered(1))


def _sigmoid(x):
    return 1.0 / (1.0 + jnp.exp(-x))


def _gelu_tanh(x):
    return 0.5 * x * (1.0 + jnp.tanh(math.sqrt(2.0 / math.pi) * (x + 0.044715 * (x * x * x))))


def _layer_norm(y, g, b):
    mu = jnp.mean(y, axis=-1, keepdims=True)
    yc = y - mu
    var = jnp.mean(yc * yc, axis=-1, keepdims=True)
    return yc * lax.rsqrt(var + LN_EPS) * g + b


def _dot(a, b):
    return jnp.dot(a, b, preferred_element_type=F32)


def _inproj_kernel(x_ref, w_ref, *o_refs, chunk, batch_major_in):
    if batch_major_in:
        nb, ts, d = x_ref.shape
        x = pltpu.einshape("bsd->sbd", x_ref[...]).reshape(ts * nb, d)
        o_refs[-1][...] = x
        o_refs = o_refs[:-1]
    else:
        x = x_ref[...]
    xb = x.astype(BF16)
    col = 0
    for o_ref in o_refs:
        width = o_ref.shape[1]
        for c0 in range(0, width, chunk):
            c1 = min(c0 + chunk, width)
            o_ref[:, c0:c1] = _dot(xb, w_ref[:, col + c0:col + c1]).astype(o_ref.dtype)
        col += width


def _inproj(x, w, outs, tm, batch_major_in):
    if batch_major_in:
        nb, seq, d = x.shape
        t = nb * seq
        x_spec = pl.BlockSpec((nb, tm // nb, d), lambda i: (0, i, 0))
        outs = tuple(outs) + ((d, F32),)
    else:
        t, d = x.shape
        x_spec = pl.BlockSpec((tm, d), lambda i: (i, 0))
    return pl.pallas_call(
        functools.partial(_inproj_kernel, chunk=512, batch_major_in=batch_major_in),
        grid=(t // tm,),
        in_specs=[x_spec, _resident(w.shape)],
        out_specs=[pl.BlockSpec((tm, wd), lambda i: (i, 0)) for wd, _ in outs],
        out_shape=[jax.ShapeDtypeStruct((t, wd), dt) for wd, dt in outs],
        compiler_params=_params("parallel"),
        name="inproj",
    )(x, w)


def _lru_kernel(xa_ref, halo_ref, ga_ref, cw_ref, cb_ref, wx_ref, bx_ref, wa_ref, ba_ref, l_ref,
                y_ref, ext_s, a_s, h_s, carry_s, *, nb, tm):
    i = pl.program_id(1)
    halo_rows = (LRU_CONV - 1) * nb

    @pl.when(i == 0)
    def _():
        carry_s[...] = jnp.zeros_like(carry_s)

    halo = halo_ref[halo_ref.shape[0] - halo_rows:, :]
    ext_s[0:halo_rows, :] = jnp.where(i > 0, halo, 0.0)
    ext_s[halo_rows:, :] = xa_ref[...]
    xc = cb_ref[...]
    for k in range(LRU_CONV):
        xc = xc + cw_ref[k:k + 1, :] * ext_s[k * nb:k * nb + tm, :]

    xcb = xc.astype(BF16)
    i_gate = _sigmoid(_dot(xcb, wx_ref[0]) + bx_ref[...])
    r_gate = _sigmoid(_dot(xcb, wa_ref[0]) + ba_ref[...])
    neg_l = -l_ref[...]
    softplus = jnp.maximum(neg_l, 0.0) + jnp.log1p(jnp.exp(-jnp.abs(neg_l)))
    log_a = (-LRU_C * softplus) * r_gate
    a = jnp.exp(log_a)
    a_s[...] = a
    h_s[...] = jnp.sqrt(-jnp.tanh(log_a) * (a * a + 1.0)) * (i_gate * xc)

    def step(t, h):
        r0 = pl.multiple_of(t * nb, nb)
        h = a_s[pl.ds(r0, nb), :] * h + h_s[pl.ds(r0, nb), :]
        h_s[pl.ds(r0, nb), :] = h
        return h

    carry_s[...] = lax.fori_loop(0, tm // nb, step, carry_s[...], unroll=4)
    y_ref[...] = (h_s[...] * _gelu_tanh(ga_ref[...])).astype(y_ref.dtype)


def _rg_lru(h_a, conv_w, conv_b, wx4, bx, wa4, ba, lam, nb, tm, halo_blk):
    t = h_a.shape[0]
    nc = LRU_WIDTH // LRU_CHUNK
    c = LRU_CHUNK
    row = lambda cc, i: (0, cc)
    return pl.pallas_call(
        functools.partial(_lru_kernel, nb=nb, tm=tm),
        grid=(nc, t // tm),
        in_specs=[
            pl.BlockSpec((tm, c), lambda cc, i: (i, cc)),
            pl.BlockSpec((halo_blk, c), lambda cc, i: (jnp.maximum(i * (tm // halo_blk) - 1, 0), cc)),
            pl.BlockSpec((tm, c), lambda cc, i: (i, nc + cc)),
            pl.BlockSpec((LRU_CONV, c), row),
            pl.BlockSpec((1, c), row),
            pl.BlockSpec((1, c, c), lambda cc, i: (cc, 0, 0)),
            pl.BlockSpec((1, c), row),
            pl.BlockSpec((1, c, c), lambda cc, i: (cc, 0, 0)),
            pl.BlockSpec((1, c), row),
            pl.BlockSpec((1, c), row),
        ],
        out_specs=pl.BlockSpec((tm, c), lambda cc, i: (i, cc)),
        out_shape=jax.ShapeDtypeStruct((t, LRU_WIDTH), BF16),
        scratch_shapes=[
            pltpu.VMEM((tm + (LRU_CONV - 1) * nb, c), F32),
            pltpu.VMEM((tm, c), F32),
            pltpu.VMEM((tm, c), F32),
            pltpu.VMEM((nb, c), F32),
        ],
        compiler_params=_params("parallel", "arbitrary"),
        name="rg_lru",
    )(h_a, h_a, h_a, conv_w, conv_b, wx4, bx, wa4, ba, lam)


def _paired_head_order(n_kvh, g):
    order = []
    for p in range(n_kvh // 2):
        for j in range(g):
            order += [(2 * p) * g + j, (2 * p + 1) * g + j]
    return order


def _head_columns(order):
    return jnp.concatenate([jnp.arange(h * HEAD_DIM, (h + 1) * HEAD_DIM) for h in order])


def _attn_kernel(*refs, g, tq, max_dist, has_sinks, has_lse):
    refs = list(refs)
    sink_ref = refs.pop(0) if has_sinks else None
    q_ref, kc_ref, vc_ref, kp_ref, vp_ref, o_ref = refs[:6]
    refs = refs[6:]
    lse_ref = refs.pop(0) if has_lse else None
    q_s, k_s, v_s, o_s = refs[:4]
    l_s = refs[4] if has_lse else None
    i = pl.program_id(0)
    pair = pl.program_id(2)
    ns = q_ref.shape[1]
    nblk = tq // BLOCK
    half = LANES // 2
    nrb = 2 * g

    q_s[...] = pltpu.einshape("tnw->ntw", q_ref[...])
    k_s[:, 0:BLOCK, :] = pltpu.einshape("tnw->ntw", kp_ref[...])
    k_s[:, BLOCK:, :] = pltpu.einshape("tnw->ntw", kc_ref[...])
    v_s[:, 0:BLOCK, :] = pltpu.einshape("tnw->ntw", vp_ref[...])
    v_s[:, BLOCK:, :] = pltpu.einshape("tnw->ntw", vc_ref[...])

    lo = lax.broadcasted_iota(jnp.int32, (1, LANES), 1) < half
    qi = lax.broadcasted_iota(jnp.int32, (BLOCK, 2 * BLOCK), 0)
    kj = lax.broadcasted_iota(jnp.int32, (BLOCK, 2 * BLOCK), 1)
    dist = qi + BLOCK - kj
    band = (dist >= 0) & (dist <= max_dist)
    bias_any = jnp.where(band, 0.0, MASK_BIAS).astype(F32)
    bias_first = jnp.where(band & (kj >= BLOCK), 0.0, MASK_BIAS).astype(F32)
    ones = jnp.ones((2 * BLOCK, LANES), BF16)
    if has_sinks:
        sinks = [sink_ref[(2 * pair + (rb % 2)) * g + rb // 2] for rb in range(nrb)]

    def seq_body(n, carry):
        for jb in range(nblk):
            qrows = slice(jb * BLOCK, (jb + 1) * BLOCK)
            krows = slice(jb * BLOCK, (jb + 2) * BLOCK)
            parts = []
            for j in range(g):
                qj = q_s[n, qrows, j * LANES:(j + 1) * LANES]
                zero = jnp.zeros_like(qj)
                parts += [jnp.where(lo, qj, zero), jnp.where(lo, zero, qj)]
            lhs = jnp.concatenate(parts, axis=0)
            s = lax.dot_general(lhs, k_s[n, krows, :], (((1,), (1,)), ((), ())),
                                preferred_element_type=F32)
            bias = jnp.where(i * nblk + jb > 0, bias_any, bias_first)
            m, p = [], []
            for rb in range(nrb):
                s_rb = s[rb * BLOCK:(rb + 1) * BLOCK, :] + bias
                m_rb = jnp.max(s_rb, axis=-1, keepdims=True)
                if has_sinks:
                    m_rb = jnp.maximum(m_rb, sinks[rb])
                m.append(m_rb)
                p.append(jnp.exp(s_rb - m_rb).astype(BF16))
            vones = jnp.concatenate([v_s[n, krows, :], ones], axis=1)
            od = _dot(jnp.concatenate(p, axis=0), vones)
            den = []
            for rb in range(nrb):
                d_rb = od[rb * BLOCK:(rb + 1) * BLOCK, LANES:2 * LANES]
                if has_sinks:
                    d_rb = d_rb + jnp.exp(sinks[rb] - m[rb])
                den.append(d_rb)
            for j in range(g):
                ra, rb = 2 * j, 2 * j + 1
                val = jnp.where(lo, od[ra * BLOCK:(ra + 1) * BLOCK, 0:LANES],
                                od[rb * BLOCK:(rb + 1) * BLOCK, 0:LANES])
                dsel = jnp.where(lo, den[ra], den[rb])
                o_s[n, qrows, j * LANES:(j + 1) * LANES] = (val / dsel).astype(o_s.dtype)
                if has_lse:
                    l_s[n, qrows, j * LANES:(j + 1) * LANES] = (
                        jnp.where(lo, m[ra], m[rb]) + jnp.log(dsel))
        return carry

    lax.fori_loop(0, ns, seq_body, 0, unroll=2)
    o_ref[...] = pltpu.einshape("ntw->tnw", o_s[...])
    if has_lse:
        lse_ref[...] = pltpu.einshape("ntw->tnw", l_s[...])


def _banded_attention(q3, kv3, *, q_blk0, g, n_kvh, max_dist, tq, sinks=None, has_lse=False,
                      out_dtype=BF16):
    length, nseq, _ = q3.shape
    ns = min(ATTN_SEQS, nseq)
    npair = n_kvh // 2
    qbw = g * LANES
    rpb = tq // BLOCK
    cur = lambda off: (lambda i, n, p: (i, n, off + p))
    prev = lambda off: (lambda i, n, p: (jnp.maximum(i * rpb - 1, 0), n, off + p))
    in_specs = [
        pl.BlockSpec((tq, ns, qbw), cur(q_blk0)),
        pl.BlockSpec((tq, ns, LANES), cur(0)),
        pl.BlockSpec((tq, ns, LANES), cur(npair)),
        pl.BlockSpec((BLOCK, ns, LANES), prev(0)),
        pl.BlockSpec((BLOCK, ns, LANES), prev(npair)),
    ]
    args = [q3, kv3, kv3, kv3, kv3]
    if sinks is not None:
        in_specs.insert(0, pl.BlockSpec(memory_space=pltpu.SMEM))
        args.insert(0, sinks)
    ow = n_kvh * g * HEAD_DIM
    out_shape = [jax.ShapeDtypeStruct((length, nseq, ow), out_dtype)]
    out_specs = [pl.BlockSpec((tq, ns, qbw), cur(0))]
    scratch = [
        pltpu.VMEM((ns, tq, qbw), BF16),
        pltpu.VMEM((ns, tq + BLOCK, LANES), BF16),
        pltpu.VMEM((ns, tq + BLOCK, LANES), BF16),
        pltpu.VMEM((ns, tq, qbw), out_dtype),
    ]
    if has_lse:
        out_shape.append(jax.ShapeDtypeStruct((length, nseq, ow), F32))
        out_specs.append(pl.BlockSpec((tq, ns, qbw), cur(0)))
        scratch.append(pltpu.VMEM((ns, tq, qbw), F32))
    return pl.pallas_call(
        functools.partial(_attn_kernel, g=g, tq=tq, max_dist=max_dist,
                          has_sinks=sinks is not None, has_lse=has_lse),
        grid=(length // tq, nseq // ns, npair),
        in_specs=in_specs,
        out_specs=out_specs,
        out_shape=out_shape,
        scratch_shapes=scratch,
        compiler_params=_params("parallel", "parallel", "parallel"),
        name="banded_attn",
    )(*args)


def _s5_kernel(u_ref, bd_ref, lre_ref, lim_ref, cd_ref, d_ref, gw_ref, gb_ref, y_ref,
               sb_s, st_s, *, nb, tm):
    hw = S5_HALF

    @pl.when(pl.program_id(0) == 0)
    def _():
        st_s[...] = jnp.zeros_like(st_s)

    zs = []
    for j in range(S5_NBLK):
        u = u_ref[:, j * LANES:(j + 1) * LANES]
        sb_s[j] = _dot(u.astype(BF16), bd_ref[j])
        lre = jnp.broadcast_to(lre_ref[j], (nb, hw))
        lim = jnp.broadcast_to(lim_ref[j], (nb, hw))
        sre = st_s[j, :, 0:hw]
        sim = st_s[j, :, hw:2 * hw]
        for t in range(tm // nb):
            rows = slice(t * nb, (t + 1) * nb)
            sre, sim = (lre * sre - lim * sim + sb_s[j, rows, 0:hw],
                        lre * sim + lim * sre + sb_s[j, rows, hw:2 * hw])
            sb_s[j, rows, 0:hw] = sre
            sb_s[j, rows, hw:2 * hw] = sim
        st_s[j, :, 0:hw] = sre
        st_s[j, :, hw:2 * hw] = sim
        cs = _dot(sb_s[j].astype(BF16), cd_ref[j])
        y = cs[:, 0:LANES] + cs[:, LANES:2 * LANES] + d_ref[:, j * LANES:(j + 1) * LANES] * u
        zs.append(_gelu_tanh(y))
    z = jnp.concatenate(zs, axis=1)
    gate = _sigmoid(_dot(z.astype(BF16), gw_ref[...]) + gb_ref[...])
    y_ref[...] = (z * gate).astype(y_ref.dtype)


def _s5(u, bd, lre, lim, cd, dsk, glu_w, glu_b, nb, tm):
    t = u.shape[0]
    return pl.pallas_call(
        functools.partial(_s5_kernel, nb=nb, tm=tm),
        grid=(t // tm,),
        in_specs=[pl.BlockSpec((tm, S5_WIDTH), lambda i: (i, 0))]
        + [_resident(a.shape) for a in (bd, lre, lim, cd, dsk, glu_w, glu_b)],
        out_specs=pl.BlockSpec((tm, S5_WIDTH), lambda i: (i, 0)),
        out_shape=jax.ShapeDtypeStruct((t, S5_WIDTH), BF16),
        scratch_shapes=[
            pltpu.VMEM((S5_NBLK, tm, 2 * S5_HALF), F32),
            pltpu.VMEM((S5_NBLK, nb, 2 * S5_HALF), F32),
        ],
        compiler_params=_params("arbitrary"),
        name="s5",
    )(u, bd, lre, lim, cd, dsk, glu_w, glu_b)


def _outproj_even_kernel(ya_ref, ob_ref, x_ref, w_ref, g_ref, b_ref, o_ref, *, rows):
    k1 = ya_ref.shape[1]
    k2 = ob_ref.shape[1]
    for r0 in range(0, x_ref.shape[0], rows):
        rs = slice(r0, r0 + rows)
        mix = _dot(ya_ref[rs, :], w_ref[0:k1, :]) + _dot(ob_ref[rs, :], w_ref[k1:k1 + k2, :])
        o_ref[rs, :] = _layer_norm(ALPHA * x_ref[rs, :] + mix, g_ref[...], b_ref[...])


def _outproj_odd_kernel(yc_ref, o0_ref, o1_ref, o2_ref, l0_ref, l1_ref, l2_ref, x_ref, w_ref,
                        g_ref, b_ref, o_ref, *, rows):
    k1 = yc_ref.shape[1]
    k2 = o0_ref.shape[1]
    for r0 in range(0, x_ref.shape[0], rows):
        rs = slice(r0, r0 + rows)
        l0, l1, l2 = l0_ref[rs, :], l1_ref[rs, :], l2_ref[rs, :]
        m = jnp.maximum(jnp.maximum(l0, l1), l2)
        e0, e1, e2 = jnp.exp(l0 - m), jnp.exp(l1 - m), jnp.exp(l2 - m)
        yd = (e0 * o0_ref[rs, :] + e1 * o1_ref[rs, :] + e2 * o2_ref[rs, :]) / (e0 + e1 + e2)
        mix = _dot(yc_ref[rs, :], w_ref[0:k1, :]) + _dot(yd.astype(BF16), w_ref[k1:k1 + k2, :])
        o_ref[rs, :] = _layer_norm(ALPHA * x_ref[rs, :] + mix, g_ref[...], b_ref[...])


def _outproj_ln(kernel_fn, acts, x, w, ln_g, ln_b, tm, name):
    t, d = x.shape
    row = lambda i: (i, 0)
    return pl.pallas_call(
        functools.partial(kernel_fn, rows=256),
        grid=(t // tm,),
        in_specs=[pl.BlockSpec((tm, a.shape[1]), row) for a in acts]
        + [pl.BlockSpec((tm, d), row), _resident(w.shape), _resident((1, d)), _resident((1, d))],
        out_specs=pl.BlockSpec((tm, d), row),
        out_shape=jax.ShapeDtypeStruct((t, d), F32),
        compiler_params=_params("parallel"),
        name=name,
    )(*acts, x, w, ln_g, ln_b)


def _ffn_kernel(x_ref, wg_ref, wv_ref, cwg_ref, cwv_ref, cbg_ref, cbv_ref, wd_ref, g_ref, b_ref,
                o_ref, xb_s, hg_s, hv_s, cg_s, cv_s, acc_s, *, nb, tm, nj, batch_major_out):
    i = pl.program_id(0)
    j = pl.program_id(1)
    halo = (FFN_CONV - 1) * nb

    @pl.when(j == 0)
    def _():
        xb_s[...] = x_ref[...].astype(BF16)
        acc_s[...] = jnp.zeros_like(acc_s)

    @pl.when(i == 0)
    def _():
        cg_s[j] = jnp.zeros(cg_s.shape[1:], F32)
        cv_s[j] = jnp.zeros(cv_s.shape[1:], F32)

    hg_s[0:halo, :] = cg_s[j]
    hv_s[0:halo, :] = cv_s[j]
    hg_s[halo:, :] = _dot(xb_s[...], wg_ref[...])
    hv_s[halo:, :] = _dot(xb_s[...], wv_ref[...])
    cg_s[j] = hg_s[tm:tm + halo, :]
    cv_s[j] = hv_s[tm:tm + halo, :]

    gate = cbg_ref[...]
    val = cbv_ref[...]
    for k in range(FFN_CONV):
        gate = gate + cwg_ref[k:k + 1, :] * hg_s[k * nb:k * nb + tm, :]
        val = val + cwv_ref[k:k + 1, :] * hv_s[k * nb:k * nb + tm, :]
    act = (gate * _sigmoid(gate) * val).astype(BF16)
    acc_s[...] += _dot(act, wd_ref[...])

    @pl.when(j == nj - 1)
    def _():
        y = _layer_norm(ALPHA * x_ref[...] + acc_s[...], g_ref[...], b_ref[...])
        if batch_major_out:
            y = pltpu.einshape("sbd->bsd", y.reshape(tm // nb, nb, y.shape[1]))
        o_ref[...] = y


def _conv_ffn_ln(x, w_up, conv_w, conv_b, w_down, ln_g, ln_b, nb, tm, tf, batch_major_out):
    t, d = x.shape
    nj = D_FF // tf
    halo = (FFN_CONV - 1) * nb
    if batch_major_out:
        out_spec = pl.BlockSpec((nb, tm // nb, d), lambda i, j: (0, i, 0))
        out_shape = jax.ShapeDtypeStruct((nb, t // nb, d), F32)
    else:
        out_spec = pl.BlockSpec((tm, d), lambda i, j: (i, 0))
        out_shape = jax.ShapeDtypeStruct((t, d), F32)
    return pl.pallas_call(
        functools.partial(_ffn_kernel, nb=nb, tm=tm, nj=nj, batch_major_out=batch_major_out),
        grid=(t // tm, nj),
        in_specs=[
            pl.BlockSpec((tm, d), lambda i, j: (i, 0)),
            pl.BlockSpec((d, tf), lambda i, j: (0, j)),
            pl.BlockSpec((d, tf), lambda i, j: (0, nj + j)),
            pl.BlockSpec((FFN_CONV, tf), lambda i, j: (0, j)),
            pl.BlockSpec((FFN_CONV, tf), lambda i, j: (0, nj + j)),
            pl.BlockSpec((1, tf), lambda i, j: (0, j)),
            pl.BlockSpec((1, tf), lambda i, j: (0, nj + j)),
            pl.BlockSpec((tf, d), lambda i, j: (j, 0)),
            _resident(ln_g.shape),
            _resident(ln_b.shape),
        ],
        out_specs=out_spec,
        out_shape=out_shape,
        scratch_shapes=[
            pltpu.VMEM((tm, d), BF16),
            pltpu.VMEM((tm + halo, tf), F32),
            pltpu.VMEM((tm + halo, tf), F32),
            pltpu.VMEM((nj, halo, tf), F32),
            pltpu.VMEM((nj, halo, tf), F32),
            pltpu.VMEM((tm, d), F32),
        ],
        compiler_params=_params("arbitrary", "arbitrary"),
        name="conv_ffn_ln",
    )(x, w_up, w_up, conv_w, conv_w, conv_b, conv_b, w_down, ln_g, ln_b)


def _row(v):
    return v.astype(F32).reshape(1, -1)


def _block_diag(w, per):
    n, c, d = w.shape
    eye = jnp.eye(per, dtype=w.dtype)
    out = jnp.einsum('qncd,nm->qncmd', w.reshape(n // per, per, c, d), eye)
    return out.reshape(n // per, per * c, per * d)


def _s5_matrices(a_re, a_im, log_dt, b_re, b_im, c_re, c_im):
    a_re, a_im = a_re.astype(F32), a_im.astype(F32)
    dt = jnp.exp(log_dt.astype(F32))[:, None]
    mag = jnp.exp(a_re * dt)
    lre = mag * jnp.cos(a_im * dt)
    lim = mag * jnp.sin(a_im * dt)
    den = a_re * a_re + a_im * a_im
    cre = ((lre - 1.0) * a_re + lim * a_im) / den
    cim = (lim * a_re - (lre - 1.0) * a_im) / den
    bb_re = cre[:, :, None] * b_re - cim[:, :, None] * b_im
    bb_im = cre[:, :, None] * b_im + cim[:, :, None] * b_re
    eye = jnp.eye(S5_GPB, dtype=F32)
    shp = (S5_NBLK, S5_GPB, S5_STATE, S5_GROUP)
    bd = jnp.concatenate([
        jnp.einsum('jgpc,gh->jgchp', m.reshape(shp), eye).reshape(S5_NBLK, LANES, S5_HALF)
        for m in (bb_re, bb_im)], axis=2)
    shc = (S5_NBLK, S5_GPB, S5_GROUP, S5_STATE)
    cd_re, cd_im = [
        jnp.einsum('jgcp,gh->jgphc', m.reshape(shc), eye).reshape(S5_NBLK, S5_HALF, LANES)
        for m in (c_re.astype(F32), -c_im.astype(F32))]
    zero = jnp.zeros_like(cd_re)
    cd = jnp.concatenate([jnp.concatenate([cd_re, zero], axis=2),
                          jnp.concatenate([zero, cd_im], axis=2)], axis=1)
    return (bd.astype(BF16), lre.reshape(S5_NBLK, 1, S5_HALF), lim.reshape(S5_NBLK, 1, S5_HALF),
            cd.astype(BF16))


def _even_layer(x, w_in, conv_w, conv_b, gx_w, gx_b, ga_w, ga_b, lru_l, sinks, w_out,
                ln1_g, ln1_b, ffn_up, ffn_conv_w, ffn_conv_b, ffn_down, ln2_g, ln2_b, *, tm):
    bsz, seq, d = x.shape
    t = bsz * seq
    g = SWA_HEADS // SWA_KV_HEADS
    qw = SWA_HEADS * HEAD_DIM
    kw = SWA_KV_HEADS * HEAD_DIM
    qcols = _head_columns(_paired_head_order(SWA_KV_HEADS, g))
    wf = w_in.astype(F32)
    o_q = 2 * LRU_WIDTH
    w_in_b = jnp.concatenate(
        [wf[:, :o_q], wf[:, o_q:o_q + qw][:, qcols] * HEAD_DIM ** -0.5, wf[:, o_q + qw:]],
        axis=1).astype(BF16)
    outs = ((2 * LRU_WIDTH, F32), (qw, BF16), (2 * kw, BF16))
    h_a, h_q, h_kv, x_tm = _inproj(x, w_in_b, outs, tm, True)

    per = LRU_CHUNK // (LRU_WIDTH // LRU_BLOCKS)
    ya = _rg_lru(h_a, conv_w.astype(F32), _row(conv_b),
                 _block_diag(gx_w.astype(F32), per).astype(BF16), _row(gx_b),
                 _block_diag(ga_w.astype(F32), per).astype(BF16), _row(ga_b), _row(lru_l),
                 bsz, tm, 4 * bsz)

    ob, = _banded_attention(
        h_q.reshape(seq, bsz, qw), h_kv.reshape(seq, bsz, 2 * kw), q_blk0=0, g=g,
        n_kvh=SWA_KV_HEADS, max_dist=SWA_WINDOW - 1, tq=2 * BLOCK, sinks=sinks.astype(F32))

    w_out_b = jnp.concatenate([w_out[:LRU_WIDTH], w_out[LRU_WIDTH:][qcols]], axis=0).astype(BF16)
    x1 = _outproj_ln(_outproj_even_kernel, [ya, ob.reshape(t, qw)], x_tm, w_out_b,
                     _row(ln1_g), _row(ln1_b), tm, "outproj_even")
    return _conv_ffn_ln(x1, ffn_up.astype(BF16), ffn_conv_w.astype(F32), _row(ffn_conv_b),
                        ffn_down.astype(BF16), _row(ln2_g), _row(ln2_b), bsz, tm, 512, False)


def _odd_layer(x, bsz, w_in, a_re, a_im, log_dt, b_re, b_im, c_re, c_im, d_skip, glu_w, glu_b,
               w_out, ln1_g, ln1_b, ffn_up, ffn_conv_w, ffn_conv_b, ffn_down, ln2_g, ln2_b, *, tm):
    t, d = x.shape
    seq = t // bsz
    ncfg = len(DIL_CONFIGS)
    g = DIL_HEADS // DIL_KV_HEADS
    qw = DIL_HEADS * HEAD_DIM
    kw = DIL_KV_HEADS * HEAD_DIM
    qcols = _head_columns(_paired_head_order(DIL_KV_HEADS, g))
    wf = w_in.astype(F32)
    o_q = S5_WIDTH
    w_q = jnp.concatenate([wf[:, o_q + r * qw:o_q + (r + 1) * qw][:, qcols] for r in range(ncfg)],
                          axis=1) * HEAD_DIM ** -0.5
    w_in_b = jnp.concatenate([wf[:, :o_q], w_q, wf[:, o_q + ncfg * qw:]], axis=1).astype(BF16)
    outs = ((S5_WIDTH, F32), (ncfg * qw, BF16), (2 * kw, BF16))
    u, h_q, h_kv = _inproj(x, w_in_b, outs, tm, False)

    bd, lre, lim, cd = _s5_matrices(a_re, a_im, log_dt, b_re, b_im, c_re, c_im)
    yc = _s5(u, bd, lre, lim, cd, _row(d_skip), glu_w.astype(BF16), _row(glu_b), bsz, tm)

    o_l = []
    for r, (window, dil) in enumerate(DIL_CONFIGS):
        length = seq // dil
        nseq = dil * bsz
        o, lse = _banded_attention(
            h_q.reshape(length, nseq, ncfg * qw), h_kv.reshape(length, nseq, 2 * kw),
            q_blk0=r * (DIL_KV_HEADS // 2), g=g, n_kvh=DIL_KV_HEADS, max_dist=window // dil,
            tq=min(length, 2 * BLOCK), has_lse=True, out_dtype=F32)
        o_l.append((o.reshape(t, qw), lse.reshape(t, qw)))

    w_out_b = jnp.concatenate([w_out[:S5_WIDTH], w_out[S5_WIDTH:][qcols]], axis=0).astype(BF16)
    x1 = _outproj_ln(_outproj_odd_kernel, [yc] + [o for o, _ in o_l] + [l for _, l in o_l], x,
                     w_out_b, _row(ln1_g), _row(ln1_b), tm, "outproj_odd")
    return _conv_ffn_ln(x1, ffn_up.astype(BF16), ffn_conv_w.astype(F32), _row(ffn_conv_b),
                        ffn_down.astype(BF16), _row(ln2_g), _row(ln2_b), bsz, tm, 512, True)


def kernel(x, l0_w_in, l0_lru_conv_w, l0_lru_conv_b, l0_lru_gx_w, l0_lru_gx_b, l0_lru_ga_w, l0_lru_ga_b, l0_lru_L, l0_sinks, l0_w_out, l0_ln1_g, l0_ln1_b, l0_ffn_up, l0_ffn_conv_w, l0_ffn_conv_b, l0_ffn_down, l0_ln2_g, l0_ln2_b, l1_w_in, l1_s5_A_re, l1_s5_A_im, l1_s5_log_dt, l1_s5_B_re, l1_s5_B_im, l1_s5_C_re, l1_s5_C_im, l1_s5_D, l1_glu_w, l1_glu_b, l1_w_out, l1_ln1_g, l1_ln1_b, l1_ffn_up, l1_ffn_conv_w, l1_ffn_conv_b, l1_ffn_down, l1_ln2_g, l1_ln2_b):
    bsz, seq, d = x.shape
    assert seq % (DIL_CONFIGS[-1][1] * BLOCK) == 0 and bsz % 8 == 0
    tm = 32 * bsz
    h = _even_layer(x, l0_w_in, l0_lru_conv_w, l0_lru_conv_b, l0_lru_gx_w, l0_lru_gx_b, l0_lru_ga_w,
                    l0_lru_ga_b, l0_lru_L, l0_sinks, l0_w_out, l0_ln1_g, l0_ln1_b, l0_ffn_up,
                    l0_ffn_conv_w, l0_ffn_conv_b, l0_ffn_down, l0_ln2_g, l0_ln2_b, tm=tm)
    return _odd_layer(h, bsz, l1_w_in, l1_s5_A_re, l1_s5_A_im, l1_s5_log_dt, l1_s5_B_re, l1_s5_B_im,
                      l1_s5_C_re, l1_s5_C_im, l1_s5_D, l1_glu_w, l1_glu_b, l1_w_out, l1_ln1_g, l1_ln1_b,
                      l1_ffn_up, l1_ffn_conv_w, l1_ffn_conv_b, l1_ffn_down, l1_ln2_g, l1_ln2_b, tm=tm)
```

```python
import functools
import math

import jax
import jax.numpy as jnp
from jax import lax
from jax.experimental import pallas as pl
from jax.experimental.pallas import tpu as pltpu

F32 = jnp.float32
BF16 = jnp.bfloat16

HEAD_DIM = 64
BLOCK = 128
LANES = 128
LRU_WIDTH = 1024
LRU_BLOCKS = 16
LRU_CONV = 4
LRU_C = 8.0
LRU_CHUNK = 256
SWA_HEADS = 16
SWA_KV_HEADS = 4
SWA_WINDOW = 128
S5_WIDTH = 768
S5_GROUP = 16
S5_GROUPS = S5_WIDTH // S5_GROUP
S5_STATE = 64
S5_GPB = LANES // S5_GROUP
S5_NBLK = S5_WIDTH // LANES
S5_HALF = S5_GPB * S5_STATE
DIL_CONFIGS = ((128, 1), (512, 4), (2048, 16))
DIL_HEADS = 8
DIL_KV_HEADS = 4
D_FF = 5632
FFN_CONV = 3
FFN_ROW_PARTS = 2
DEPTH = 2
ALPHA = (2 * DEPTH) ** 0.25
LN_EPS = 1e-5
MASK_BIAS = -1e30
ATTN_SEQS = 16

V7X_VMEM_LIMIT_BYTES = 56 * 1024 * 1024


def _params(*sem):
    return pltpu.CompilerParams(dimension_semantics=sem, vmem_limit_bytes=V7X_VMEM_LIMIT_BYTES)


def _resident(shape):
    nd = len(shape)
    return pl.BlockSpec(shape, lambda *_: (0,) * nd, pipeline_mode=pl.Buffered(1))


def _sigmoid(x):
    return 0.5 + 0.5 * jnp.tanh(0.5 * x)


def _gelu_tanh(x):
    return 0.5 * x * (1.0 + jnp.tanh(math.sqrt(2.0 / math.pi) * (x + 0.044715 * (x * x * x))))


def _layer_norm(y, g, b):
    mu = jnp.mean(y, axis=-1, keepdims=True)
    yc = y - mu
    var = jnp.mean(yc * yc, axis=-1, keepdims=True)
    return yc * lax.rsqrt(var + LN_EPS) * g + b


def _dot(a, b):
    return jnp.dot(a, b, preferred_element_type=F32)


def _inproj_kernel(x_ref, w_ref, *o_refs, chunk, batch_major_in):
    if batch_major_in:
        nb, ts, d = x_ref.shape
        x = pltpu.einshape("bsd->sbd", x_ref[...]).reshape(ts * nb, d)
        o_refs[-1][...] = x
        o_refs = o_refs[:-1]
    else:
        x = x_ref[...]
    xb = x.astype(BF16)
    col = 0
    for o_ref in o_refs:
        width = o_ref.shape[1]
        for c0 in range(0, width, chunk):
            c1 = min(c0 + chunk, width)
            o_ref[:, c0:c1] = _dot(xb, w_ref[:, col + c0:col + c1]).astype(o_ref.dtype)
        col += width


def _inproj(x, w, outs, tm, batch_major_in):
    if batch_major_in:
        nb, seq, d = x.shape
        t = nb * seq
        x_spec = pl.BlockSpec((nb, tm // nb, d), lambda i: (0, i, 0))
        outs = tuple(outs) + ((d, F32),)
    else:
        t, d = x.shape
        x_spec = pl.BlockSpec((tm, d), lambda i: (i, 0))
    return pl.pallas_call(
        functools.partial(_inproj_kernel, chunk=512, batch_major_in=batch_major_in),
        grid=(t // tm,),
        in_specs=[x_spec, _resident(w.shape)],
        out_specs=[pl.BlockSpec((tm, wd), lambda i: (i, 0)) for wd, _ in outs],
        out_shape=[jax.ShapeDtypeStruct((t, wd), dt) for wd, dt in outs],
        compiler_params=_params("parallel"),
        name="inproj",
    )(x, w)


def _lru_kernel(xa_ref, halo_ref, ga_ref, cw_ref, cb_ref, wx_ref, bx_ref, wa_ref, ba_ref, l_ref,
                y_ref, ext_s, a_s, h_s, carry_s, *, nb, tm):
    i = pl.program_id(1)
    halo_rows = (LRU_CONV - 1) * nb

    @pl.when(i == 0)
    def _():
        carry_s[...] = jnp.zeros_like(carry_s)

    halo = halo_ref[halo_ref.shape[0] - halo_rows:, :]
    ext_s[0:halo_rows, :] = jnp.where(i > 0, halo, 0.0)
    ext_s[halo_rows:, :] = xa_ref[...]
    xc = cb_ref[...]
    for k in range(LRU_CONV):
        xc = xc + cw_ref[k:k + 1, :] * ext_s[k * nb:k * nb + tm, :]

    xcb = xc.astype(BF16)
    i_gate = _sigmoid(_dot(xcb, wx_ref[0]) + bx_ref[...])
    r_gate = _sigmoid(_dot(xcb, wa_ref[0]) + ba_ref[...])
    neg_l = -l_ref[...]
    softplus = jnp.maximum(neg_l, 0.0) + jnp.log1p(jnp.exp(-jnp.abs(neg_l)))
    log_a = (-LRU_C * softplus) * r_gate
    a = jnp.exp(log_a)
    a_s[...] = a
    om = 1.0 - a * a
    h_s[...] = jnp.where(om > 0.0, om * lax.rsqrt(om), 0.0) * (i_gate * xc)

    def step(t, h):
        r0 = pl.multiple_of(t * nb, nb)
        h = a_s[pl.ds(r0, nb), :] * h + h_s[pl.ds(r0, nb), :]
        h_s[pl.ds(r0, nb), :] = h
        return h

    carry_s[...] = lax.fori_loop(0, tm // nb, step, carry_s[...], unroll=4)
    y_ref[...] = (h_s[...] * _gelu_tanh(ga_ref[...])).astype(y_ref.dtype)


def _rg_lru(h_a, conv_w, conv_b, wx4, bx, wa4, ba, lam, nb, tm, halo_blk):
    t = h_a.shape[0]
    nc = LRU_WIDTH // LRU_CHUNK
    c = LRU_CHUNK
    row = lambda cc, i: (0, cc)
    return pl.pallas_call(
        functools.partial(_lru_kernel, nb=nb, tm=tm),
        grid=(nc, t // tm),
        in_specs=[
            pl.BlockSpec((tm, c), lambda cc, i: (i, cc)),
            pl.BlockSpec((halo_blk, c), lambda cc, i: (jnp.maximum(i * (tm // halo_blk) - 1, 0), cc)),
            pl.BlockSpec((tm, c), lambda cc, i: (i, nc + cc)),
            pl.BlockSpec((LRU_CONV, c), row),
            pl.BlockSpec((1, c), row),
            pl.BlockSpec((1, c, c), lambda cc, i: (cc, 0, 0)),
            pl.BlockSpec((1, c), row),
            pl.BlockSpec((1, c, c), lambda cc, i: (cc, 0, 0)),
            pl.BlockSpec((1, c), row),
            pl.BlockSpec((1, c), row),
        ],
        out_specs=pl.BlockSpec((tm, c), lambda cc, i: (i, cc)),
        out_shape=jax.ShapeDtypeStruct((t, LRU_WIDTH), BF16),
        scratch_shapes=[
            pltpu.VMEM((tm + (LRU_CONV - 1) * nb, c), F32),
            pltpu.VMEM((tm, c), F32),
            pltpu.VMEM((tm, c), F32),
            pltpu.VMEM((nb, c), F32),
        ],
        compiler_params=_params("parallel", "arbitrary"),
        name="rg_lru",
    )(h_a, h_a, h_a, conv_w, conv_b, wx4, bx, wa4, ba, lam)


def _paired_head_order(n_kvh, g):
    order = []
    for p in range(n_kvh // 2):
        for j in range(g):
            order += [(2 * p) * g + j, (2 * p + 1) * g + j]
    return order


def _head_columns(order):
    return jnp.concatenate([jnp.arange(h * HEAD_DIM, (h + 1) * HEAD_DIM) for h in order])


def _attn_kernel(*refs, g, tq, max_dist, has_sinks, has_lse):
    refs = list(refs)
    sink_ref = refs.pop(0) if has_sinks else None
    q_ref, kc_ref, vc_ref, kp_ref, vp_ref, o_ref = refs[:6]
    refs = refs[6:]
    lse_ref = refs.pop(0) if has_lse else None
    q_s, k_s, v_s, o_s = refs[:4]
    l_s = refs[4] if has_lse else None
    i = pl.program_id(0)
    pair = pl.program_id(2)
    ns = q_ref.shape[1]
    nblk = tq // BLOCK
    half = LANES // 2
    nrb = 2 * g

    q_s[...] = pltpu.einshape("tnw->ntw", q_ref[...])
    k_s[:, 0:BLOCK, :] = pltpu.einshape("tnw->ntw", kp_ref[...])
    k_s[:, BLOCK:, :] = pltpu.einshape("tnw->ntw", kc_ref[...])
    v_s[:, 0:BLOCK, :] = pltpu.einshape("tnw->ntw", vp_ref[...])
    v_s[:, BLOCK:, :] = pltpu.einshape("tnw->ntw", vc_ref[...])

    lo = lax.broadcasted_iota(jnp.int32, (1, LANES), 1) < half
    qi = lax.broadcasted_iota(jnp.int32, (BLOCK, 2 * BLOCK), 0)
    kj = lax.broadcasted_iota(jnp.int32, (BLOCK, 2 * BLOCK), 1)
    dist = qi + BLOCK - kj
    band = (dist >= 0) & (dist <= max_dist)
    bias_any = jnp.where(band, 0.0, MASK_BIAS).astype(F32)
    bias_first = jnp.where(band & (kj >= BLOCK), 0.0, MASK_BIAS).astype(F32)
    ones = jnp.ones((2 * BLOCK, LANES), BF16)
    if has_sinks:
        sinks = [sink_ref[(2 * pair + (rb % 2)) * g + rb // 2] for rb in range(nrb)]

    def seq_body(n, carry):
        for jb in range(nblk):
            qrows = slice(jb * BLOCK, (jb + 1) * BLOCK)
            krows = slice(jb * BLOCK, (jb + 2) * BLOCK)
            parts = []
            for j in range(g):
                qj = q_s[n, qrows, j * LANES:(j + 1) * LANES]
                zero = jnp.zeros_like(qj)
                parts += [jnp.where(lo, qj, zero), jnp.where(lo, zero, qj)]
            lhs = jnp.concatenate(parts, axis=0)
            s = lax.dot_general(lhs, k_s[n, krows, :], (((1,), (1,)), ((), ())),
                                preferred_element_type=F32)
            bias = jnp.where(i * nblk + jb > 0, bias_any, bias_first)
            m, p = [], []
            for rb in range(nrb):
                s_rb = s[rb * BLOCK:(rb + 1) * BLOCK, :] + bias
                m_rb = jnp.max(s_rb, axis=-1, keepdims=True)
                if has_sinks:
                    m_rb = jnp.maximum(m_rb, sinks[rb])
                m.append(m_rb)
                p.append(jnp.exp(s_rb - m_rb).astype(BF16))
            vones = jnp.concatenate([v_s[n, krows, :], ones], axis=1)
            od = _dot(jnp.concatenate(p, axis=0), vones)
            den = []
            for rb in range(nrb):
                d_rb = od[rb * BLOCK:(rb + 1) * BLOCK, LANES:2 * LANES]
                if has_sinks:
                    d_rb = d_rb + jnp.exp(sinks[rb] - m[rb])
                den.append(d_rb)
            for j in range(g):
                ra, rb = 2 * j, 2 * j + 1
                val = jnp.where(lo, od[ra * BLOCK:(ra + 1) * BLOCK, 0:LANES],
                                od[rb * BLOCK:(rb + 1) * BLOCK, 0:LANES])
                dsel = jnp.where(lo, den[ra], den[rb])
                o_s[n, qrows, j * LANES:(j + 1) * LANES] = (val / dsel).astype(o_s.dtype)
                if has_lse:
                    l_s[n, qrows, j * LANES:(j + 1) * LANES] = (
                        jnp.where(lo, m[ra], m[rb]) + jnp.log(dsel))
        return carry

    lax.fori_loop(0, ns, seq_body, 0, unroll=min(ns, 4 // nblk))
    o_ref[...] = pltpu.einshape("ntw->tnw", o_s[...])
    if has_lse:
        lse_ref[...] = pltpu.einshape("ntw->tnw", l_s[...])


def _banded_attention(q3, kv3, *, q_blk0, g, n_kvh, max_dist, tq, sinks=None, has_lse=False,
                      out_dtype=BF16):
    length, nseq, _ = q3.shape
    ns = min(ATTN_SEQS, nseq)
    npair = n_kvh // 2
    qbw = g * LANES
    rpb = tq // BLOCK
    cur = lambda off: (lambda i, n, p: (i, n, off + p))
    prev = lambda off: (lambda i, n, p: (jnp.maximum(i * rpb - 1, 0), n, off + p))
    in_specs = [
        pl.BlockSpec((tq, ns, qbw), cur(q_blk0)),
        pl.BlockSpec((tq, ns, LANES), cur(0)),
        pl.BlockSpec((tq, ns, LANES), cur(npair)),
        pl.BlockSpec((BLOCK, ns, LANES), prev(0)),
        pl.BlockSpec((BLOCK, ns, LANES), prev(npair)),
    ]
    args = [q3, kv3, kv3, kv3, kv3]
    if sinks is not None:
        in_specs.insert(0, pl.BlockSpec(memory_space=pltpu.SMEM))
        args.insert(0, sinks)
    ow = n_kvh * g * HEAD_DIM
    out_shape = [jax.ShapeDtypeStruct((length, nseq, ow), out_dtype)]
    out_specs = [pl.BlockSpec((tq, ns, qbw), cur(0))]
    scratch = [
        pltpu.VMEM((ns, tq, qbw), BF16),
        pltpu.VMEM((ns, tq + BLOCK, LANES), BF16),
        pltpu.VMEM((ns, tq + BLOCK, LANES), BF16),
        pltpu.VMEM((ns, tq, qbw), out_dtype),
    ]
    if has_lse:
        out_shape.append(jax.ShapeDtypeStruct((length, nseq, ow), F32))
        out_specs.append(pl.BlockSpec((tq, ns, qbw), cur(0)))
        scratch.append(pltpu.VMEM((ns, tq, qbw), F32))
    return pl.pallas_call(
        functools.partial(_attn_kernel, g=g, tq=tq, max_dist=max_dist,
                          has_sinks=sinks is not None, has_lse=has_lse),
        grid=(length // tq, nseq // ns, npair),
        in_specs=in_specs,
        out_specs=out_specs,
        out_shape=out_shape,
        scratch_shapes=scratch,
        compiler_params=_params("parallel", "parallel", "parallel"),
        name="banded_attn",
    )(*args)


def _s5_kernel(u_ref, bd_ref, lre_ref, lim_ref, cd_ref, d_ref, gw_ref, gb_ref, y_ref,
               sb_s, st_s, *, nb, tm):
    hw = S5_HALF

    @pl.when(pl.program_id(0) == 0)
    def _():
        st_s[...] = jnp.zeros_like(st_s)

    zs = []
    for j in range(S5_NBLK):
        u = u_ref[:, j * LANES:(j + 1) * LANES]
        sb_s[j] = _dot(u.astype(BF16), bd_ref[j])
        lre = jnp.broadcast_to(lre_ref[j], (nb, hw))
        lim = jnp.broadcast_to(lim_ref[j], (nb, hw))
        sre = st_s[j, :, 0:hw]
        sim = st_s[j, :, hw:2 * hw]
        for t in range(tm // nb):
            rows = slice(t * nb, (t + 1) * nb)
            sre, sim = (lre * sre - lim * sim + sb_s[j, rows, 0:hw],
                        lre * sim + lim * sre + sb_s[j, rows, hw:2 * hw])
            sb_s[j, rows, 0:hw] = sre
            sb_s[j, rows, hw:2 * hw] = sim
        st_s[j, :, 0:hw] = sre
        st_s[j, :, hw:2 * hw] = sim
        cs = _dot(sb_s[j].astype(BF16), cd_ref[j])
        y = cs[:, 0:LANES] + cs[:, LANES:2 * LANES] + d_ref[:, j * LANES:(j + 1) * LANES] * u
        zs.append(_gelu_tanh(y))
    z = jnp.concatenate(zs, axis=1)
    gate = _sigmoid(_dot(z.astype(BF16), gw_ref[...]) + gb_ref[...])
    y_ref[...] = (z * gate).astype(y_ref.dtype)


def _s5(u, bd, lre, lim, cd, dsk, glu_w, glu_b, nb, tm):
    t = u.shape[0]
    return pl.pallas_call(
        functools.partial(_s5_kernel, nb=nb, tm=tm),
        grid=(t // tm,),
        in_specs=[pl.BlockSpec((tm, S5_WIDTH), lambda i: (i, 0))]
        + [_resident(a.shape) for a in (bd, lre, lim, cd, dsk, glu_w, glu_b)],
        out_specs=pl.BlockSpec((tm, S5_WIDTH), lambda i: (i, 0)),
        out_shape=jax.ShapeDtypeStruct((t, S5_WIDTH), BF16),
        scratch_shapes=[
            pltpu.VMEM((S5_NBLK, tm, 2 * S5_HALF), F32),
            pltpu.VMEM((S5_NBLK, nb, 2 * S5_HALF), F32),
        ],
        compiler_params=_params("arbitrary"),
        name="s5",
    )(u, bd, lre, lim, cd, dsk, glu_w, glu_b)


def _outproj_even_kernel(ya_ref, ob_ref, x_ref, w_ref, g_ref, b_ref, o_ref, *, rows):
    k1 = ya_ref.shape[1]
    k2 = ob_ref.shape[1]
    for r0 in range(0, x_ref.shape[0], rows):
        rs = slice(r0, r0 + rows)
        mix = _dot(ya_ref[rs, :], w_ref[0:k1, :]) + _dot(ob_ref[rs, :], w_ref[k1:k1 + k2, :])
        o_ref[rs, :] = _layer_norm(ALPHA * x_ref[rs, :] + mix, g_ref[...], b_ref[...])


def _outproj_odd_kernel(yc_ref, o0_ref, o1_ref, o2_ref, l0_ref, l1_ref, l2_ref, x_ref, w_ref,
                        g_ref, b_ref, o_ref, *, rows):
    k1 = yc_ref.shape[1]
    k2 = o0_ref.shape[1]
    for r0 in range(0, x_ref.shape[0], rows):
        rs = slice(r0, r0 + rows)
        l0, l1, l2 = l0_ref[rs, :], l1_ref[rs, :], l2_ref[rs, :]
        m = jnp.maximum(jnp.maximum(l0, l1), l2)
        e0, e1, e2 = jnp.exp(l0 - m), jnp.exp(l1 - m), jnp.exp(l2 - m)
        yd = (e0 * o0_ref[rs, :] + e1 * o1_ref[rs, :] + e2 * o2_ref[rs, :]) / (e0 + e1 + e2)
        mix = _dot(yc_ref[rs, :], w_ref[0:k1, :]) + _dot(yd.astype(BF16), w_ref[k1:k1 + k2, :])
        o_ref[rs, :] = _layer_norm(ALPHA * x_ref[rs, :] + mix, g_ref[...], b_ref[...])


def _outproj_ln(kernel_fn, acts, x, w, ln_g, ln_b, tm, name):
    t, d = x.shape
    row = lambda i: (i, 0)
    return pl.pallas_call(
        functools.partial(kernel_fn, rows=256),
        grid=(t // tm,),
        in_specs=[pl.BlockSpec((tm, a.shape[1]), row) for a in acts]
        + [pl.BlockSpec((tm, d), row), _resident(w.shape), _resident((1, d)), _resident((1, d))],
        out_specs=pl.BlockSpec((tm, d), row),
        out_shape=jax.ShapeDtypeStruct((t, d), F32),
        compiler_params=_params("parallel"),
        name=name,
    )(*acts, x, w, ln_g, ln_b)


def _ffn_kernel(x_ref, wg_ref, wv_ref, cwg_ref, cwv_ref, cbg_ref, cbv_ref, wd_ref, g_ref, b_ref,
                o_ref, xb_s, hg_s, hv_s, cg_s, cv_s, acc_s, *, nb, tm, nj, batch_major_out):
    i = pl.program_id(0)
    j = pl.program_id(1)
    halo = (FFN_CONV - 1) * nb

    @pl.when(j == 0)
    def _():
        xb_s[...] = x_ref[...].astype(BF16)
        acc_s[...] = jnp.zeros_like(acc_s)

    @pl.when(i == 0)
    def _():
        cg_s[j] = jnp.zeros(cg_s.shape[1:], F32)
        cv_s[j] = jnp.zeros(cv_s.shape[1:], F32)

    hg_s[0:halo, :] = cg_s[j]
    hv_s[0:halo, :] = cv_s[j]
    rp = tm // FFN_ROW_PARTS
    for r0 in range(0, tm, rp):
        hg_s[halo + r0:halo + r0 + rp, :] = _dot(xb_s[r0:r0 + rp, :], wg_ref[...])
        hv_s[halo + r0:halo + r0 + rp, :] = _dot(xb_s[r0:r0 + rp, :], wv_ref[...])
    cg_s[j] = hg_s[tm:tm + halo, :]
    cv_s[j] = hv_s[tm:tm + halo, :]

    for r0 in range(0, tm, rp):
        gate = cbg_ref[...]
        val = cbv_ref[...]
        for k in range(FFN_CONV):
            gate = gate + cwg_ref[k:k + 1, :] * hg_s[k * nb + r0:k * nb + r0 + rp, :]
            val = val + cwv_ref[k:k + 1, :] * hv_s[k * nb + r0:k * nb + r0 + rp, :]
        act = (gate * _sigmoid(gate) * val).astype(BF16)
        acc_s[r0:r0 + rp, :] += _dot(act, wd_ref[...])

    @pl.when(j == nj - 1)
    def _():
        y = _layer_norm(ALPHA * x_ref[...] + acc_s[...], g_ref[...], b_ref[...])
        if batch_major_out:
            y = pltpu.einshape("sbd->bsd", y.reshape(tm // nb, nb, y.shape[1]))
        o_ref[...] = y


def _conv_ffn_ln(x, w_up, conv_w, conv_b, w_down, ln_g, ln_b, nb, tm, tf, batch_major_out):
    t, d = x.shape
    nj = D_FF // tf
    halo = (FFN_CONV - 1) * nb
    if batch_major_out:
        out_spec = pl.BlockSpec((nb, tm // nb, d), lambda i, j: (0, i, 0))
        out_shape = jax.ShapeDtypeStruct((nb, t // nb, d), F32)
    else:
        out_spec = pl.BlockSpec((tm, d), lambda i, j: (i, 0))
        out_shape = jax.ShapeDtypeStruct((t, d), F32)
    return pl.pallas_call(
        functools.partial(_ffn_kernel, nb=nb, tm=tm, nj=nj, batch_major_out=batch_major_out),
        grid=(t // tm, nj),
        in_specs=[
            pl.BlockSpec((tm, d), lambda i, j: (i, 0)),
            pl.BlockSpec((d, tf), lambda i, j: (0, j)),
            pl.BlockSpec((d, tf), lambda i, j: (0, nj + j)),
            pl.BlockSpec((FFN_CONV, tf), lambda i, j: (0, j)),
            pl.BlockSpec((FFN_CONV, tf), lambda i, j: (0, nj + j)),
            pl.BlockSpec((1, tf), lambda i, j: (0, j)),
            pl.BlockSpec((1, tf), lambda i, j: (0, nj + j)),
            pl.BlockSpec((tf, d), lambda i, j: (j, 0)),
            _resident(ln_g.shape),
            _resident(ln_b.shape),
        ],
        out_specs=out_spec,
        out_shape=out_shape,
        scratch_shapes=[
            pltpu.VMEM((tm, d), BF16),
            pltpu.VMEM((tm + halo, tf), F32),
            pltpu.VMEM((tm + halo, tf), F32),
            pltpu.VMEM((nj, halo, tf), F32),
            pltpu.VMEM((nj, halo, tf), F32),
            pltpu.VMEM((tm, d), F32),
        ],
        compiler_params=_params("arbitrary", "arbitrary"),
        name="conv_ffn_ln",
    )(x, w_up, w_up, conv_w, conv_w, conv_b, conv_b, w_down, ln_g, ln_b)


def _row(v):
    return v.astype(F32).reshape(1, -1)


def _block_diag(w, per):
    n, c, d = w.shape
    eye = jnp.eye(per, dtype=w.dtype)
    out = jnp.einsum('qncd,nm->qncmd', w.reshape(n // per, per, c, d), eye)
    return out.reshape(n // per, per * c, per * d)


def _s5_matrices(a_re, a_im, log_dt, b_re, b_im, c_re, c_im):
    a_re, a_im = a_re.astype(F32), a_im.astype(F32)
    dt = jnp.exp(log_dt.astype(F32))[:, None]
    mag = jnp.exp(a_re * dt)
    lre = mag * jnp.cos(a_im * dt)
    lim = mag * jnp.sin(a_im * dt)
    den = a_re * a_re + a_im * a_im
    cre = ((lre - 1.0) * a_re + lim * a_im) / den
    cim = (lim * a_re - (lre - 1.0) * a_im) / den
    bb_re = cre[:, :, None] * b_re - cim[:, :, None] * b_im
    bb_im = cre[:, :, None] * b_im + cim[:, :, None] * b_re
    eye = jnp.eye(S5_GPB, dtype=F32)
    shp = (S5_NBLK, S5_GPB, S5_STATE, S5_GROUP)
    bd = jnp.concatenate([
        jnp.einsum('jgpc,gh->jgchp', m.reshape(shp), eye).reshape(S5_NBLK, LANES, S5_HALF)
        for m in (bb_re, bb_im)], axis=2)
    shc = (S5_NBLK, S5_GPB, S5_GROUP, S5_STATE)
    cd_re, cd_im = [
        jnp.einsum('jgcp,gh->jgphc', m.reshape(shc), eye).reshape(S5_NBLK, S5_HALF, LANES)
        for m in (c_re.astype(F32), -c_im.astype(F32))]
    zero = jnp.zeros_like(cd_re)
    cd = jnp.concatenate([jnp.concatenate([cd_re, zero], axis=2),
                          jnp.concatenate([zero, cd_im], axis=2)], axis=1)
    return (bd.astype(BF16), lre.reshape(S5_NBLK, 1, S5_HALF), lim.reshape(S5_NBLK, 1, S5_HALF),
            cd.astype(BF16))


def _even_layer(x, w_in, conv_w, conv_b, gx_w, gx_b, ga_w, ga_b, lru_l, sinks, w_out,
                ln1_g, ln1_b, ffn_up, ffn_conv_w, ffn_conv_b, ffn_down, ln2_g, ln2_b, *, tm):
    bsz, seq, d = x.shape
    t = bsz * seq
    g = SWA_HEADS // SWA_KV_HEADS
    qw = SWA_HEADS * HEAD_DIM
    kw = SWA_KV_HEADS * HEAD_DIM
    qcols = _head_columns(_paired_head_order(SWA_KV_HEADS, g))
    wf = w_in.astype(F32)
    o_q = 2 * LRU_WIDTH
    w_in_b = jnp.concatenate(
        [wf[:, :o_q], wf[:, o_q:o_q + qw][:, qcols] * HEAD_DIM ** -0.5, wf[:, o_q + qw:]],
        axis=1).astype(BF16)
    outs = ((2 * LRU_WIDTH, F32), (qw, BF16), (2 * kw, BF16))
    h_a, h_q, h_kv, x_tm = _inproj(x, w_in_b, outs, tm, True)

    per = LRU_CHUNK // (LRU_WIDTH // LRU_BLOCKS)
    ya = _rg_lru(h_a, conv_w.astype(F32), _row(conv_b),
                 _block_diag(gx_w.astype(F32), per).astype(BF16), _row(gx_b),
                 _block_diag(ga_w.astype(F32), per).astype(BF16), _row(ga_b), _row(lru_l),
                 bsz, min(2 * tm, t), 4 * bsz)

    ob, = _banded_attention(
        h_q.reshape(seq, bsz, qw), h_kv.reshape(seq, bsz, 2 * kw), q_blk0=0, g=g,
        n_kvh=SWA_KV_HEADS, max_dist=SWA_WINDOW - 1, tq=2 * BLOCK, sinks=sinks.astype(F32))

    w_out_b = jnp.concatenate([w_out[:LRU_WIDTH], w_out[LRU_WIDTH:][qcols]], axis=0).astype(BF16)
    x1 = _outproj_ln(_outproj_even_kernel, [ya, ob.reshape(t, qw)], x_tm, w_out_b,
                     _row(ln1_g), _row(ln1_b), tm, "outproj_even")
    return _conv_ffn_ln(x1, ffn_up.astype(BF16), ffn_conv_w.astype(F32), _row(ffn_conv_b),
                        ffn_down.astype(BF16), _row(ln2_g), _row(ln2_b), bsz, tm, 512, False)


def _odd_layer(x, bsz, w_in, a_re, a_im, log_dt, b_re, b_im, c_re, c_im, d_skip, glu_w, glu_b,
               w_out, ln1_g, ln1_b, ffn_up, ffn_conv_w, ffn_conv_b, ffn_down, ln2_g, ln2_b, *, tm):
    t, d = x.shape
    seq = t // bsz
    ncfg = len(DIL_CONFIGS)
    g = DIL_HEADS // DIL_KV_HEADS
    qw = DIL_HEADS * HEAD_DIM
    kw = DIL_KV_HEADS * HEAD_DIM
    qcols = _head_columns(_paired_head_order(DIL_KV_HEADS, g))
    wf = w_in.astype(F32)
    o_q = S5_WIDTH
    w_q = jnp.concatenate([wf[:, o_q + r * qw:o_q + (r + 1) * qw][:, qcols] for r in range(ncfg)],
                          axis=1) * HEAD_DIM ** -0.5
    w_in_b = jnp.concatenate([wf[:, :o_q], w_q, wf[:, o_q + ncfg * qw:]], axis=1).astype(BF16)
    outs = ((S5_WIDTH, F32), (ncfg * qw, BF16), (2 * kw, BF16))
    u, h_q, h_kv = _inproj(x, w_in_b, outs, tm, False)

    bd, lre, lim, cd = _s5_matrices(a_re, a_im, log_dt, b_re, b_im, c_re, c_im)
    yc = _s5(u, bd, lre, lim, cd, _row(d_skip), glu_w.astype(BF16), _row(glu_b), bsz, tm)

    o_l = []
    for r, (window, dil) in enumerate(DIL_CONFIGS):
        length = seq // dil
        nseq = dil * bsz
        o, lse = _banded_attention(
            h_q.reshape(length, nseq, ncfg * qw), h_kv.reshape(length, nseq, 2 * kw),
            q_blk0=r * (DIL_KV_HEADS // 2), g=g, n_kvh=DIL_KV_HEADS, max_dist=window // dil,
            tq=min(length, 2 * BLOCK), has_lse=True, out_dtype=F32)
        o_l.append((o.reshape(t, qw), lse.reshape(t, qw)))

    w_out_b = jnp.concatenate([w_out[:S5_WIDTH], w_out[S5_WIDTH:][qcols]], axis=0).astype(BF16)
    x1 = _outproj_ln(_outproj_odd_kernel, [yc] + [o for o, _ in o_l] + [l for _, l in o_l], x,
                     w_out_b, _row(ln1_g), _row(ln1_b), tm, "outproj_odd")
    return _conv_ffn_ln(x1, ffn_up.astype(BF16), ffn_conv_w.astype(F32), _row(ffn_conv_b),
                        ffn_down.astype(BF16), _row(ln2_g), _row(ln2_b), bsz, tm, 512, True)


def kernel(x, l0_w_in, l0_lru_conv_w, l0_lru_conv_b, l0_lru_gx_w, l0_lru_gx_b, l0_lru_ga_w, l0_lru_ga_b, l0_lru_L, l0_sinks, l0_w_out, l0_ln1_g, l0_ln1_b, l0_ffn_up, l0_ffn_conv_w, l0_ffn_conv_b, l0_ffn_down, l0_ln2_g, l0_ln2_b, l1_w_in, l1_s5_A_re, l1_s5_A_im, l1_s5_log_dt, l1_s5_B_re, l1_s5_B_im, l1_s5_C_re, l1_s5_C_im, l1_s5_D, l1_glu_w, l1_glu_b, l1_w_out, l1_ln1_g, l1_ln1_b, l1_ffn_up, l1_ffn_conv_w, l1_ffn_conv_b, l1_ffn_down, l1_ln2_g, l1_ln2_b):
    bsz, seq, d = x.shape
    assert seq % (DIL_CONFIGS[-1][1] * BLOCK) == 0 and bsz % 8 == 0
    tm = 32 * bsz
    h = _even_layer(x, l0_w_in, l0_lru_conv_w, l0_lru_conv_b, l0_lru_gx_w, l0_lru_gx_b, l0_lru_ga_w,
                    l0_lru_ga_b, l0_lru_L, l0_sinks, l0_w_out, l0_ln1_g, l0_ln1_b, l0_ffn_up,
                    l0_ffn_conv_w, l0_ffn_conv_b, l0_ffn_down, l0_ln2_g, l0_ln2_b, tm=tm)
    return _odd_layer(h, bsz, l1_w_in, l1_s5_A_re, l1_s5_A_im, l1_s5_log_dt, l1_s5_B_re, l1_s5_B_im,
                      l1_s5_C_re, l1_s5_C_im, l1_s5_D, l1_glu_w, l1_glu_b, l1_w_out, l1_ln1_g, l1_ln1_b,
                      l1_ffn_up, l1_ffn_conv_w, l1_ffn_conv_b, l1_ffn_down, l1_ln2_g, l1_ln2_b, tm=tm)
```

```python
import functools
import math

import jax
import jax.numpy as jnp
from jax import lax
from jax.experimental import pallas as pl
from jax.experimental.pallas import tpu as pltpu

F32 = jnp.float32
BF16 = jnp.bfloat16

HEAD_DIM = 64
BLOCK = 128
LANES = 128
LRU_WIDTH = 1024
LRU_BLOCKS = 16
LRU_CONV = 4
LRU_C = 8.0
LRU_CHUNK = 256
SWA_HEADS = 16
SWA_KV_HEADS = 4
SWA_WINDOW = 128
S5_WIDTH = 768
S5_GROUP = 16
S5_GROUPS = S5_WIDTH // S5_GROUP
S5_STATE = 64
S5_GPB = LANES // S5_GROUP
S5_NBLK = S5_WIDTH // LANES
S5_HALF = S5_GPB * S5_STATE
DIL_CONFIGS = ((128, 1), (512, 4), (2048, 16))
DIL_HEADS = 8
DIL_KV_HEADS = 4
D_FF = 5632
FFN_CONV = 3
DEPTH = 2
ALPHA = (2 * DEPTH) ** 0.25
LN_EPS = 1e-5
MASK_BIAS = -1e30
ATTN_SEQS = 16

V7X_VMEM_LIMIT_BYTES = 56 * 1024 * 1024


def _params(*sem):
    return pltpu.CompilerParams(dimension_semantics=sem, vmem_limit_bytes=V7X_VMEM_LIMIT_BYTES)


def _resident(shape):
    nd = len(shape)
    return pl.BlockSpec(shape, lambda *_: (0,) * nd, pipeline_mode=pl.Buffered(1))


def _sigmoid(x):
    return 0.5 + 0.5 * jnp.tanh(0.5 * x)


def _gelu_tanh(x):
    return 0.5 * x * (1.0 + jnp.tanh(math.sqrt(2.0 / math.pi) * (x + 0.044715 * (x * x * x))))


def _layer_norm(y, g, b):
    mu = jnp.mean(y, axis=-1, keepdims=True)
    yc = y - mu
    var = jnp.mean(yc * yc, axis=-1, keepdims=True)
    return yc * lax.rsqrt(var + LN_EPS) * g + b


def _dot(a, b):
    return jnp.dot(a, b, preferred_element_type=F32)


def _inproj_kernel(x_ref, w_ref, *o_refs, chunk, batch_major_in):
    if batch_major_in:
        nb, ts, d = x_ref.shape
        x = pltpu.einshape("bsd->sbd", x_ref[...]).reshape(ts * nb, d)
        o_refs[-1][...] = x
        o_refs = o_refs[:-1]
    else:
        x = x_ref[...]
    xb = x.astype(BF16)
    col = 0
    for o_ref in o_refs:
        width = o_ref.shape[1]
        for c0 in range(0, width, chunk):
            c1 = min(c0 + chunk, width)
            o_ref[:, c0:c1] = _dot(xb, w_ref[:, col + c0:col + c1]).astype(o_ref.dtype)
        col += width


def _inproj(x, w, outs, tm, batch_major_in):
    if batch_major_in:
        nb, seq, d = x.shape
        t = nb * seq
        x_spec = pl.BlockSpec((nb, tm // nb, d), lambda i: (0, i, 0))
        outs = tuple(outs) + ((d, F32),)
    else:
        t, d = x.shape
        x_spec = pl.BlockSpec((tm, d), lambda i: (i, 0))
    return pl.pallas_call(
        functools.partial(_inproj_kernel, chunk=512, batch_major_in=batch_major_in),
        grid=(t // tm,),
        in_specs=[x_spec, _resident(w.shape)],
        out_specs=[pl.BlockSpec((tm, wd), lambda i: (i, 0)) for wd, _ in outs],
        out_shape=[jax.ShapeDtypeStruct((t, wd), dt) for wd, dt in outs],
        compiler_params=_params("parallel"),
        name="inproj",
    )(x, w)


def _lru_kernel(xa_ref, halo_ref, ga_ref, cw_ref, cb_ref, wx_ref, bx_ref, wa_ref, ba_ref, l_ref,
                y_ref, ext_s, a_s, h_s, carry_s, *, nb, tm):
    i = pl.program_id(1)
    halo_rows = (LRU_CONV - 1) * nb

    @pl.when(i == 0)
    def _():
        carry_s[...] = jnp.zeros_like(carry_s)

    halo = halo_ref[halo_ref.shape[0] - halo_rows:, :]
    ext_s[0:halo_rows, :] = jnp.where(i > 0, halo, 0.0)
    ext_s[halo_rows:, :] = xa_ref[...]
    xc = cb_ref[...]
    for k in range(LRU_CONV):
        xc = xc + cw_ref[k:k + 1, :] * ext_s[k * nb:k * nb + tm, :]

    xcb = xc.astype(BF16)
    i_gate = _sigmoid(_dot(xcb, wx_ref[0]) + bx_ref[...])
    r_gate = _sigmoid(_dot(xcb, wa_ref[0]) + ba_ref[...])
    neg_l = -l_ref[...]
    softplus = jnp.maximum(neg_l, 0.0) + jnp.log1p(jnp.exp(-jnp.abs(neg_l)))
    log_a = (-LRU_C * softplus) * r_gate
    a = jnp.exp(log_a)
    a_s[...] = a
    om = 1.0 - a * a
    h_s[...] = jnp.where(om > 0.0, om * lax.rsqrt(om), 0.0) * (i_gate * xc)

    def step(t, h):
        r0 = pl.multiple_of(t * nb, nb)
        h = a_s[pl.ds(r0, nb), :] * h + h_s[pl.ds(r0, nb), :]
        h_s[pl.ds(r0, nb), :] = h
        return h

    carry_s[...] = lax.fori_loop(0, tm // nb, step, carry_s[...], unroll=4)
    y_ref[...] = (h_s[...] * _gelu_tanh(ga_ref[...])).astype(y_ref.dtype)


def _rg_lru(h_a, conv_w, conv_b, wx4, bx, wa4, ba, lam, nb, tm, halo_blk):
    t = h_a.shape[0]
    nc = LRU_WIDTH // LRU_CHUNK
    c = LRU_CHUNK
    row = lambda cc, i: (0, cc)
    return pl.pallas_call(
        functools.partial(_lru_kernel, nb=nb, tm=tm),
        grid=(nc, t // tm),
        in_specs=[
            pl.BlockSpec((tm, c), lambda cc, i: (i, cc)),
            pl.BlockSpec((halo_blk, c), lambda cc, i: (jnp.maximum(i * (tm // halo_blk) - 1, 0), cc)),
            pl.BlockSpec((tm, c), lambda cc, i: (i, nc + cc)),
            pl.BlockSpec((LRU_CONV, c), row),
            pl.BlockSpec((1, c), row),
            pl.BlockSpec((1, c, c), lambda cc, i: (cc, 0, 0)),
            pl.BlockSpec((1, c), row),
            pl.BlockSpec((1, c, c), lambda cc, i: (cc, 0, 0)),
            pl.BlockSpec((1, c), row),
            pl.BlockSpec((1, c), row),
        ],
        out_specs=pl.BlockSpec((tm, c), lambda cc, i: (i, cc)),
        out_shape=jax.ShapeDtypeStruct((t, LRU_WIDTH), BF16),
        scratch_shapes=[
            pltpu.VMEM((tm + (LRU_CONV - 1) * nb, c), F32),
            pltpu.VMEM((tm, c), F32),
            pltpu.VMEM((tm, c), F32),
            pltpu.VMEM((nb, c), F32),
        ],
        compiler_params=_params("parallel", "arbitrary"),
        name="rg_lru",
    )(h_a, h_a, h_a, conv_w, conv_b, wx4, bx, wa4, ba, lam)


def _paired_head_order(n_kvh, g):
    order = []
    for p in range(n_kvh // 2):
        for j in range(g):
            order += [(2 * p) * g + j, (2 * p + 1) * g + j]
    return order


def _head_columns(order):
    return jnp.concatenate([jnp.arange(h * HEAD_DIM, (h + 1) * HEAD_DIM) for h in order])


def _attn_kernel(*refs, g, tq, max_dist, has_sinks, has_lse):
    refs = list(refs)
    sink_ref = refs.pop(0) if has_sinks else None
    q_ref, kc_ref, vc_ref, kp_ref, vp_ref, o_ref = refs[:6]
    refs = refs[6:]
    lse_ref = refs.pop(0) if has_lse else None
    q_s, k_s, v_s, o_s = refs[:4]
    l_s = refs[4] if has_lse else None
    i = pl.program_id(0)
    pair = pl.program_id(2)
    ns = q_ref.shape[1]
    nblk = tq // BLOCK
    half = LANES // 2
    nrb = 2 * g

    q_s[...] = pltpu.einshape("tnw->ntw", q_ref[...])
    k_s[:, 0:BLOCK, :] = pltpu.einshape("tnw->ntw", kp_ref[...])
    k_s[:, BLOCK:, :] = pltpu.einshape("tnw->ntw", kc_ref[...])
    v_s[:, 0:BLOCK, :] = pltpu.einshape("tnw->ntw", vp_ref[...])
    v_s[:, BLOCK:, :] = pltpu.einshape("tnw->ntw", vc_ref[...])

    lo = lax.broadcasted_iota(jnp.int32, (1, LANES), 1) < half
    qi = lax.broadcasted_iota(jnp.int32, (BLOCK, 2 * BLOCK), 0)
    kj = lax.broadcasted_iota(jnp.int32, (BLOCK, 2 * BLOCK), 1)
    dist = qi + BLOCK - kj
    band = (dist >= 0) & (dist <= max_dist)
    bias_any = jnp.where(band, 0.0, MASK_BIAS).astype(F32)
    bias_first = jnp.where(band & (kj >= BLOCK), 0.0, MASK_BIAS).astype(F32)
    ones = jnp.ones((2 * BLOCK, LANES), BF16)
    if has_sinks:
        sinks = [sink_ref[(2 * pair + (rb % 2)) * g + rb // 2] for rb in range(nrb)]

    def seq_body(n, carry):
        for jb in range(nblk):
            qrows = slice(jb * BLOCK, (jb + 1) * BLOCK)
            krows = slice(jb * BLOCK, (jb + 2) * BLOCK)
            parts = []
            for j in range(g):
                qj = q_s[n, qrows, j * LANES:(j + 1) * LANES]
                zero = jnp.zeros_like(qj)
                parts += [jnp.where(lo, qj, zero), jnp.where(lo, zero, qj)]
            lhs = jnp.concatenate(parts, axis=0)
            s = lax.dot_general(lhs, k_s[n, krows, :], (((1,), (1,)), ((), ())),
                                preferred_element_type=F32)
            bias = jnp.where(i * nblk + jb > 0, bias_any, bias_first)
            m, p = [], []
            for rb in range(nrb):
                s_rb = s[rb * BLOCK:(rb + 1) * BLOCK, :] + bias
                m_rb = jnp.max(s_rb, axis=-1, keepdims=True)
                if has_sinks:
                    m_rb = jnp.maximum(m_rb, sinks[rb])
                m.append(m_rb)
                p.append(jnp.exp(s_rb - m_rb).astype(BF16))
            vones = jnp.concatenate([v_s[n, krows, :], ones], axis=1)
            od = _dot(jnp.concatenate(p, axis=0), vones)
            den = []
            for rb in range(nrb):
                d_rb = od[rb * BLOCK:(rb + 1) * BLOCK, LANES:2 * LANES]
                if has_sinks:
                    d_rb = d_rb + jnp.exp(sinks[rb] - m[rb])
                den.append(d_rb)
            for j in range(g):
                ra, rb = 2 * j, 2 * j + 1
                val = jnp.where(lo, od[ra * BLOCK:(ra + 1) * BLOCK, 0:LANES],
                                od[rb * BLOCK:(rb + 1) * BLOCK, 0:LANES])
                dsel = jnp.where(lo, den[ra], den[rb])
                o_s[n, qrows, j * LANES:(j + 1) * LANES] = (val / dsel).astype(o_s.dtype)
                if has_lse:
                    l_s[n, qrows, j * LANES:(j + 1) * LANES] = (
                        jnp.where(lo, m[ra], m[rb]) + jnp.log(dsel))
        return carry

    lax.fori_loop(0, ns, seq_body, 0, unroll=min(ns, 4 // nblk))
    o_ref[...] = pltpu.einshape("ntw->tnw", o_s[...])
    if has_lse:
        lse_ref[...] = pltpu.einshape("ntw->tnw", l_s[...])


def _banded_attention(q3, kv3, *, q_blk0, g, n_kvh, max_dist, tq, sinks=None, has_lse=False,
                      out_dtype=BF16):
    length, nseq, _ = q3.shape
    ns = min(ATTN_SEQS, nseq)
    npair = n_kvh // 2
    qbw = g * LANES
    rpb = tq // BLOCK
    cur = lambda off: (lambda i, n, p: (i, n, off + p))
    prev = lambda off: (lambda i, n, p: (jnp.maximum(i * rpb - 1, 0), n, off + p))
    in_specs = [
        pl.BlockSpec((tq, ns, qbw), cur(q_blk0)),
        pl.BlockSpec((tq, ns, LANES), cur(0)),
        pl.BlockSpec((tq, ns, LANES), cur(npair)),
        pl.BlockSpec((BLOCK, ns, LANES), prev(0)),
        pl.BlockSpec((BLOCK, ns, LANES), prev(npair)),
    ]
    args = [q3, kv3, kv3, kv3, kv3]
    if sinks is not None:
        in_specs.insert(0, pl.BlockSpec(memory_space=pltpu.SMEM))
        args.insert(0, sinks)
    ow = n_kvh * g * HEAD_DIM
    out_shape = [jax.ShapeDtypeStruct((length, nseq, ow), out_dtype)]
    out_specs = [pl.BlockSpec((tq, ns, qbw), cur(0))]
    scratch = [
        pltpu.VMEM((ns, tq, qbw), BF16),
        pltpu.VMEM((ns, tq + BLOCK, LANES), BF16),
        pltpu.VMEM((ns, tq + BLOCK, LANES), BF16),
        pltpu.VMEM((ns, tq, qbw), out_dtype),
    ]
    if has_lse:
        out_shape.append(jax.ShapeDtypeStruct((length, nseq, ow), F32))
        out_specs.append(pl.BlockSpec((tq, ns, qbw), cur(0)))
        scratch.append(pltpu.VMEM((ns, tq, qbw), F32))
    return pl.pallas_call(
        functools.partial(_attn_kernel, g=g, tq=tq, max_dist=max_dist,
                          has_sinks=sinks is not None, has_lse=has_lse),
        grid=(length // tq, nseq // ns, npair),
        in_specs=in_specs,
        out_specs=out_specs,
        out_shape=out_shape,
        scratch_shapes=scratch,
        compiler_params=_params("parallel", "parallel", "parallel"),
        name="banded_attn",
    )(*args)


def _s5_kernel(u_ref, bd_ref, lre_ref, lim_ref, cd_ref, d_ref, gw_ref, gb_ref, y_ref,
               sb_s, st_s, *, nb, tm):
    hw = S5_HALF

    @pl.when(pl.program_id(0) == 0)
    def _():
        st_s[...] = jnp.zeros_like(st_s)

    zs = []
    for j in range(S5_NBLK):
        u = u_ref[:, j * LANES:(j + 1) * LANES]
        sb_s[j] = _dot(u.astype(BF16), bd_ref[j])
        lre = jnp.broadcast_to(lre_ref[j], (nb, hw))
        lim = jnp.broadcast_to(lim_ref[j], (nb, hw))
        sre = st_s[j, :, 0:hw]
        sim = st_s[j, :, hw:2 * hw]
        for t in range(tm // nb):
            rows = slice(t * nb, (t + 1) * nb)
            sre, sim = (lre * sre - lim * sim + sb_s[j, rows, 0:hw],
                        lre * sim + lim * sre + sb_s[j, rows, hw:2 * hw])
            sb_s[j, rows, 0:hw] = sre
            sb_s[j, rows, hw:2 * hw] = sim
        st_s[j, :, 0:hw] = sre
        st_s[j, :, hw:2 * hw] = sim
        cs = _dot(sb_s[j].astype(BF16), cd_ref[j])
        y = cs[:, 0:LANES] + cs[:, LANES:2 * LANES] + d_ref[:, j * LANES:(j + 1) * LANES] * u
        zs.append(_gelu_tanh(y))
    z = jnp.concatenate(zs, axis=1)
    gate = _sigmoid(_dot(z.astype(BF16), gw_ref[...]) + gb_ref[...])
    y_ref[...] = (z * gate).astype(y_ref.dtype)


def _s5(u, bd, lre, lim, cd, dsk, glu_w, glu_b, nb, tm):
    t = u.shape[0]
    return pl.pallas_call(
        functools.partial(_s5_kernel, nb=nb, tm=tm),
        grid=(t // tm,),
        in_specs=[pl.BlockSpec((tm, S5_WIDTH), lambda i: (i, 0))]
        + [_resident(a.shape) for a in (bd, lre, lim, cd, dsk, glu_w, glu_b)],
        out_specs=pl.BlockSpec((tm, S5_WIDTH), lambda i: (i, 0)),
        out_shape=jax.ShapeDtypeStruct((t, S5_WIDTH), BF16),
        scratch_shapes=[
            pltpu.VMEM((S5_NBLK, tm, 2 * S5_HALF), F32),
            pltpu.VMEM((S5_NBLK, nb, 2 * S5_HALF), F32),
        ],
        compiler_params=_params("arbitrary"),
        name="s5",
    )(u, bd, lre, lim, cd, dsk, glu_w, glu_b)


def _outproj_even_kernel(ya_ref, ob_ref, x_ref, w_ref, g_ref, b_ref, o_ref, *, rows):
    k1 = ya_ref.shape[1]
    k2 = ob_ref.shape[1]
    for r0 in range(0, x_ref.shape[0], rows):
        rs = slice(r0, r0 + rows)
        mix = _dot(ya_ref[rs, :], w_ref[0:k1, :]) + _dot(ob_ref[rs, :], w_ref[k1:k1 + k2, :])
        o_ref[rs, :] = _layer_norm(ALPHA * x_ref[rs, :] + mix, g_ref[...], b_ref[...])


def _outproj_odd_kernel(yc_ref, o0_ref, o1_ref, o2_ref, l0_ref, l1_ref, l2_ref, x_ref, w_ref,
                        g_ref, b_ref, o_ref, *, rows):
    k1 = yc_ref.shape[1]
    k2 = o0_ref.shape[1]
    for r0 in range(0, x_ref.shape[0], rows):
        rs = slice(r0, r0 + rows)
        l0, l1, l2 = l0_ref[rs, :], l1_ref[rs, :], l2_ref[rs, :]
        m = jnp.maximum(jnp.maximum(l0, l1), l2)
        e0, e1, e2 = jnp.exp(l0 - m), jnp.exp(l1 - m), jnp.exp(l2 - m)
        yd = (e0 * o0_ref[rs, :] + e1 * o1_ref[rs, :] + e2 * o2_ref[rs, :]) / (e0 + e1 + e2)
        mix = _dot(yc_ref[rs, :], w_ref[0:k1, :]) + _dot(yd.astype(BF16), w_ref[k1:k1 + k2, :])
        o_ref[rs, :] = _layer_norm(ALPHA * x_ref[rs, :] + mix, g_ref[...], b_ref[...])


def _outproj_ln(kernel_fn, acts, x, w, ln_g, ln_b, tm, name):
    t, d = x.shape
    row = lambda i: (i, 0)
    return pl.pallas_call(
        functools.partial(kernel_fn, rows=256),
        grid=(t // tm,),
        in_specs=[pl.BlockSpec((tm, a.shape[1]), row) for a in acts]
        + [pl.BlockSpec((tm, d), row), _resident(w.shape), _resident((1, d)), _resident((1, d))],
        out_specs=pl.BlockSpec((tm, d), row),
        out_shape=jax.ShapeDtypeStruct((t, d), F32),
        compiler_params=_params("parallel"),
        name=name,
    )(*acts, x, w, ln_g, ln_b)


def _ffn_kernel(x_ref, wg_ref, wv_ref, cwg_ref, cwv_ref, cbg_ref, cbv_ref, wd_ref, g_ref, b_ref,
                o_ref, xb_s, hg_s, hv_s, cg_s, cv_s, acc_s, *, nb, tm, nj, batch_major_out):
    i = pl.program_id(0)
    j = pl.program_id(1)
    halo = (FFN_CONV - 1) * nb

    @pl.when(j == 0)
    def _():
        xb_s[...] = x_ref[...].astype(BF16)
        acc_s[...] = jnp.zeros_like(acc_s)

    @pl.when(i == 0)
    def _():
        cg_s[j] = jnp.zeros(cg_s.shape[1:], F32)
        cv_s[j] = jnp.zeros(cv_s.shape[1:], F32)

    hg_s[0:halo, :] = cg_s[j]
    hv_s[0:halo, :] = cv_s[j]
    hg_s[halo:, :] = _dot(xb_s[...], wg_ref[...])
    hv_s[halo:, :] = _dot(xb_s[...], wv_ref[...])
    cg_s[j] = hg_s[tm:tm + halo, :]
    cv_s[j] = hv_s[tm:tm + halo, :]

    gate = cbg_ref[...]
    val = cbv_ref[...]
    for k in range(FFN_CONV):
        gate = gate + cwg_ref[k:k + 1, :] * hg_s[k * nb:k * nb + tm, :]
        val = val + cwv_ref[k:k + 1, :] * hv_s[k * nb:k * nb + tm, :]
    act = (gate * _sigmoid(gate) * val).astype(BF16)
    acc_s[...] += _dot(act, wd_ref[...])

    @pl.when(j == nj - 1)
    def _():
        y = _layer_norm(ALPHA * x_ref[...] + acc_s[...], g_ref[...], b_ref[...])
        if batch_major_out:
            y = pltpu.einshape("sbd->bsd", y.reshape(tm // nb, nb, y.shape[1]))
        o_ref[...] = y


def _conv_ffn_ln(x, w_up, conv_w, conv_b, w_down, ln_g, ln_b, nb, tm, tf, batch_major_out):
    t, d = x.shape
    nj = D_FF // tf
    halo = (FFN_CONV - 1) * nb
    if batch_major_out:
        out_spec = pl.BlockSpec((nb, tm // nb, d), lambda i, j: (0, i, 0))
        out_shape = jax.ShapeDtypeStruct((nb, t // nb, d), F32)
    else:
        out_spec = pl.BlockSpec((tm, d), lambda i, j: (i, 0))
        out_shape = jax.ShapeDtypeStruct((t, d), F32)
    return pl.pallas_call(
        functools.partial(_ffn_kernel, nb=nb, tm=tm, nj=nj, batch_major_out=batch_major_out),
        grid=(t // tm, nj),
        in_specs=[
            pl.BlockSpec((tm, d), lambda i, j: (i, 0)),
            pl.BlockSpec((d, tf), lambda i, j: (0, j)),
            pl.BlockSpec((d, tf), lambda i, j: (0, nj + j)),
            pl.BlockSpec((FFN_CONV, tf), lambda i, j: (0, j)),
            pl.BlockSpec((FFN_CONV, tf), lambda i, j: (0, nj + j)),
            pl.BlockSpec((1, tf), lambda i, j: (0, j)),
            pl.BlockSpec((1, tf), lambda i, j: (0, nj + j)),
            pl.BlockSpec((tf, d), lambda i, j: (j, 0)),
            _resident(ln_g.shape),
            _resident(ln_b.shape),
        ],
        out_specs=out_spec,
        out_shape=out_shape,
        scratch_shapes=[
            pltpu.VMEM((tm, d), BF16),
            pltpu.VMEM((tm + halo, tf), F32),
            pltpu.VMEM((tm + halo, tf), F32),
            pltpu.VMEM((nj, halo, tf), F32),
            pltpu.VMEM((nj, halo, tf), F32),
            pltpu.VMEM((tm, d), F32),
        ],
        compiler_params=_params("arbitrary", "arbitrary"),
        name="conv_ffn_ln",
    )(x, w_up, w_up, conv_w, conv_w, conv_b, conv_b, w_down, ln_g, ln_b)


def _row(v):
    return v.astype(F32).reshape(1, -1)


def _block_diag(w, per):
    n, c, d = w.shape
    eye = jnp.eye(per, dtype=w.dtype)
    out = jnp.einsum('qncd,nm->qncmd', w.reshape(n // per, per, c, d), eye)
    return out.reshape(n // per, per * c, per * d)


def _s5_matrices(a_re, a_im, log_dt, b_re, b_im, c_re, c_im):
    a_re, a_im = a_re.astype(F32), a_im.astype(F32)
    dt = jnp.exp(log_dt.astype(F32))[:, None]
    mag = jnp.exp(a_re * dt)
    lre = mag * jnp.cos(a_im * dt)
    lim = mag * jnp.sin(a_im * dt)
    den = a_re * a_re + a_im * a_im
    cre = ((lre - 1.0) * a_re + lim * a_im) / den
    cim = (lim * a_re - (lre - 1.0) * a_im) / den
    bb_re = cre[:, :, None] * b_re - cim[:, :, None] * b_im
    bb_im = cre[:, :, None] * b_im + cim[:, :, None] * b_re
    eye = jnp.eye(S5_GPB, dtype=F32)
    shp = (S5_NBLK, S5_GPB, S5_STATE, S5_GROUP)
    bd = jnp.concatenate([
        jnp.einsum('jgpc,gh->jgchp', m.reshape(shp), eye).reshape(S5_NBLK, LANES, S5_HALF)
        for m in (bb_re, bb_im)], axis=2)
    shc = (S5_NBLK, S5_GPB, S5_GROUP, S5_STATE)
    cd_re, cd_im = [
        jnp.einsum('jgcp,gh->jgphc', m.reshape(shc), eye).reshape(S5_NBLK, S5_HALF, LANES)
        for m in (c_re.astype(F32), -c_im.astype(F32))]
    zero = jnp.zeros_like(cd_re)
    cd = jnp.concatenate([jnp.concatenate([cd_re, zero], axis=2),
                          jnp.concatenate([zero, cd_im], axis=2)], axis=1)
    return (bd.astype(BF16), lre.reshape(S5_NBLK, 1, S5_HALF), lim.reshape(S5_NBLK, 1, S5_HALF),
            cd.astype(BF16))


def _even_layer(x, w_in, conv_w, conv_b, gx_w, gx_b, ga_w, ga_b, lru_l, sinks, w_out,
                ln1_g, ln1_b, ffn_up, ffn_conv_w, ffn_conv_b, ffn_down, ln2_g, ln2_b, *, tm):
    bsz, seq, d = x.shape
    t = bsz * seq
    g = SWA_HEADS // SWA_KV_HEADS
    qw = SWA_HEADS * HEAD_DIM
    kw = SWA_KV_HEADS * HEAD_DIM
    qcols = _head_columns(_paired_head_order(SWA_KV_HEADS, g))
    wf = w_in.astype(F32)
    o_q = 2 * LRU_WIDTH
    w_in_b = jnp.concatenate(
        [wf[:, :o_q], wf[:, o_q:o_q + qw][:, qcols] * HEAD_DIM ** -0.5, wf[:, o_q + qw:]],
        axis=1).astype(BF16)
    outs = ((2 * LRU_WIDTH, F32), (qw, BF16), (2 * kw, BF16))
    h_a, h_q, h_kv, x_tm = _inproj(x, w_in_b, outs, tm, True)

    per = LRU_CHUNK // (LRU_WIDTH // LRU_BLOCKS)
    ya = _rg_lru(h_a, conv_w.astype(F32), _row(conv_b),
                 _block_diag(gx_w.astype(F32), per).astype(BF16), _row(gx_b),
                 _block_diag(ga_w.astype(F32), per).astype(BF16), _row(ga_b), _row(lru_l),
                 bsz, min(2 * tm, t), 4 * bsz)

    ob, = _banded_attention(
        h_q.reshape(seq, bsz, qw), h_kv.reshape(seq, bsz, 2 * kw), q_blk0=0, g=g,
        n_kvh=SWA_KV_HEADS, max_dist=SWA_WINDOW - 1, tq=2 * BLOCK, sinks=sinks.astype(F32))

    w_out_b = jnp.concatenate([w_out[:LRU_WIDTH], w_out[LRU_WIDTH:][qcols]], axis=0).astype(BF16)
    x1 = _outproj_ln(_outproj_even_kernel, [ya, ob.reshape(t, qw)], x_tm, w_out_b,
                     _row(ln1_g), _row(ln1_b), tm, "outproj_even")
    return _conv_ffn_ln(x1, ffn_up.astype(BF16), ffn_conv_w.astype(F32), _row(ffn_conv_b),
                        ffn_down.astype(BF16), _row(ln2_g), _row(ln2_b), bsz, tm, 512, False)


def _odd_layer(x, bsz, w_in, a_re, a_im, log_dt, b_re, b_im, c_re, c_im, d_skip, glu_w, glu_b,
               w_out, ln1_g, ln1_b, ffn_up, ffn_conv_w, ffn_conv_b, ffn_down, ln2_g, ln2_b, *, tm):
    t, d = x.shape
    seq = t // bsz
    ncfg = len(DIL_CONFIGS)
    g = DIL_HEADS // DIL_KV_HEADS
    qw = DIL_HEADS * HEAD_DIM
    kw = DIL_KV_HEADS * HEAD_DIM
    qcols = _head_columns(_paired_head_order(DIL_KV_HEADS, g))
    wf = w_in.astype(F32)
    o_q = S5_WIDTH
    w_q = jnp.concatenate([wf[:, o_q + r * qw:o_q + (r + 1) * qw][:, qcols] for r in range(ncfg)],
                          axis=1) * HEAD_DIM ** -0.5
    w_in_b = jnp.concatenate([wf[:, :o_q], w_q, wf[:, o_q + ncfg * qw:]], axis=1).astype(BF16)
    outs = ((S5_WIDTH, F32), (ncfg * qw, BF16), (2 * kw, BF16))
    u, h_q, h_kv = _inproj(x, w_in_b, outs, tm, False)

    bd, lre, lim, cd = _s5_matrices(a_re, a_im, log_dt, b_re, b_im, c_re, c_im)
    yc = _s5(u, bd, lre, lim, cd, _row(d_skip), glu_w.astype(BF16), _row(glu_b), bsz, tm)

    o_l = []
    for r, (window, dil) in enumerate(DIL_CONFIGS):
        length = seq // dil
        nseq = dil * bsz
        o, lse = _banded_attention(
            h_q.reshape(length, nseq, ncfg * qw), h_kv.reshape(length, nseq, 2 * kw),
            q_blk0=r * (DIL_KV_HEADS // 2), g=g, n_kvh=DIL_KV_HEADS, max_dist=window // dil,
            tq=min(length, 2 * BLOCK), has_lse=True, out_dtype=F32)
        o_l.append((o.reshape(t, qw), lse.reshape(t, qw)))

    w_out_b = jnp.concatenate([w_out[:S5_WIDTH], w_out[S5_WIDTH:][qcols]], axis=0).astype(BF16)
    x1 = _outproj_ln(_outproj_odd_kernel, [yc] + [o for o, _ in o_l] + [l for _, l in o_l], x,
                     w_out_b, _row(ln1_g), _row(ln1_b), tm, "outproj_odd")
    return _conv_ffn_ln(x1, ffn_up.astype(BF16), ffn_conv_w.astype(F32), _row(ffn_conv_b),
                        ffn_down.astype(BF16), _row(ln2_g), _row(ln2_b), bsz, tm, 512, True)


def kernel(x, l0_w_in, l0_lru_conv_w, l0_lru_conv_b, l0_lru_gx_w, l0_lru_gx_b, l0_lru_ga_w, l0_lru_ga_b, l0_lru_L, l0_sinks, l0_w_out, l0_ln1_g, l0_ln1_b, l0_ffn_up, l0_ffn_conv_w, l0_ffn_conv_b, l0_ffn_down, l0_ln2_g, l0_ln2_b, l1_w_in, l1_s5_A_re, l1_s5_A_im, l1_s5_log_dt, l1_s5_B_re, l1_s5_B_im, l1_s5_C_re, l1_s5_C_im, l1_s5_D, l1_glu_w, l1_glu_b, l1_w_out, l1_ln1_g, l1_ln1_b, l1_ffn_up, l1_ffn_conv_w, l1_ffn_conv_b, l1_ffn_down, l1_ln2_g, l1_ln2_b):
    bsz, seq, d = x.shape
    assert seq % (DIL_CONFIGS[-1][1] * BLOCK) == 0 and bsz % 8 == 0
    tm = 32 * bsz
    h = _even_layer(x, l0_w_in, l0_lru_conv_w, l0_lru_conv_b, l0_lru_gx_w, l0_lru_gx_b, l0_lru_ga_w,
                    l0_lru_ga_b, l0_lru_L, l0_sinks, l0_w_out, l0_ln1_g, l0_ln1_b, l0_ffn_up,
                    l0_ffn_conv_w, l0_ffn_conv_b, l0_ffn_down, l0_ln2_g, l0_ln2_b, tm=tm)
    return _odd_layer(h, bsz, l1_w_in, l1_s5_A_re, l1_s5_A_im, l1_s5_log_dt, l1_s5_B_re, l1_s5_B_im,
                      l1_s5_C_re, l1_s5_C_im, l1_s5_D, l1_glu_w, l1_glu_b, l1_w_out, l1_ln1_g, l1_ln1_b,
                      l1_ffn_up, l1_ffn_conv_w, l1_ffn_conv_b, l1_ffn_down, l1_ln2_g, l1_ln2_b, tm=tm)
```

```python
import functools
import math

import jax
import jax.numpy as jnp
from jax import lax
from jax.experimental import pallas as pl
from jax.experimental.pallas import tpu as pltpu

F32 = jnp.float32
BF16 = jnp.bfloat16

HEAD_DIM = 64
BLOCK = 128
LANES = 128
LRU_WIDTH = 1024
LRU_BLOCKS = 16
LRU_CONV = 4
LRU_C = 8.0
LRU_CHUNK = 256
SWA_HEADS = 16
SWA_KV_HEADS = 4
SWA_WINDOW = 128
S5_WIDTH = 768
S5_GROUP = 16
S5_GROUPS = S5_WIDTH // S5_GROUP
S5_STATE = 64
S5_GPB = LANES // S5_GROUP
S5_NBLK = S5_WIDTH // LANES
S5_HALF = S5_GPB * S5_STATE
DIL_CONFIGS = ((128, 1), (512, 4), (2048, 16))
DIL_HEADS = 8
DIL_KV_HEADS = 4
D_FF = 5632
FFN_CONV = 3
FFN_TILES_PER_STEP = 2
DEPTH = 2
ALPHA = (2 * DEPTH) ** 0.25
LN_EPS = 1e-5
MASK_BIAS = -1e30
ATTN_SEQS = 16

V7X_VMEM_LIMIT_BYTES = 56 * 1024 * 1024


def _params(*sem):
    return pltpu.CompilerParams(dimension_semantics=sem, vmem_limit_bytes=V7X_VMEM_LIMIT_BYTES)


def _resident(shape):
    nd = len(shape)
    return pl.BlockSpec(shape, lambda *_: (0,) * nd, pipeline_mode=pl.Buffered(1))


def _sigmoid(x):
    return 0.5 + 0.5 * jnp.tanh(0.5 * x)


def _gelu_tanh(x):
    return 0.5 * x * (1.0 + jnp.tanh(math.sqrt(2.0 / math.pi) * (x + 0.044715 * (x * x * x))))


def _layer_norm(y, g, b):
    mu = jnp.mean(y, axis=-1, keepdims=True)
    yc = y - mu
    var = jnp.mean(yc * yc, axis=-1, keepdims=True)
    return yc * lax.rsqrt(var + LN_EPS) * g + b


def _dot(a, b):
    return jnp.dot(a, b, preferred_element_type=F32)


def _inproj_kernel(x_ref, w_ref, *o_refs, chunk, batch_major_in):
    if batch_major_in:
        nb, ts, d = x_ref.shape
        x = pltpu.einshape("bsd->sbd", x_ref[...]).reshape(ts * nb, d)
        o_refs[-1][...] = x
        o_refs = o_refs[:-1]
    else:
        x = x_ref[...]
    xb = x.astype(BF16)
    col = 0
    for o_ref in o_refs:
        width = o_ref.shape[1]
        for c0 in range(0, width, chunk):
            c1 = min(c0 + chunk, width)
            o_ref[:, c0:c1] = _dot(xb, w_ref[:, col + c0:col + c1]).astype(o_ref.dtype)
        col += width


def _inproj(x, w, outs, tm, batch_major_in):
    if batch_major_in:
        nb, seq, d = x.shape
        t = nb * seq
        x_spec = pl.BlockSpec((nb, tm // nb, d), lambda i: (0, i, 0))
        outs = tuple(outs) + ((d, F32),)
    else:
        t, d = x.shape
        x_spec = pl.BlockSpec((tm, d), lambda i: (i, 0))
    return pl.pallas_call(
        functools.partial(_inproj_kernel, chunk=512, batch_major_in=batch_major_in),
        grid=(t // tm,),
        in_specs=[x_spec, _resident(w.shape)],
        out_specs=[pl.BlockSpec((tm, wd), lambda i: (i, 0)) for wd, _ in outs],
        out_shape=[jax.ShapeDtypeStruct((t, wd), dt) for wd, dt in outs],
        compiler_params=_params("parallel"),
        name="inproj",
    )(x, w)


def _lru_kernel(xa_ref, halo_ref, ga_ref, cw_ref, cb_ref, wx_ref, bx_ref, wa_ref, ba_ref, l_ref,
                y_ref, ext_s, a_s, h_s, carry_s, *, nb, tm):
    i = pl.program_id(1)
    halo_rows = (LRU_CONV - 1) * nb

    @pl.when(i == 0)
    def _():
        carry_s[...] = jnp.zeros_like(carry_s)

    halo = halo_ref[halo_ref.shape[0] - halo_rows:, :]
    ext_s[0:halo_rows, :] = jnp.where(i > 0, halo, 0.0)
    ext_s[halo_rows:, :] = xa_ref[...]
    xc = cb_ref[...]
    for k in range(LRU_CONV):
        xc = xc + cw_ref[k:k + 1, :] * ext_s[k * nb:k * nb + tm, :]

    xcb = xc.astype(BF16)
    i_gate = _sigmoid(_dot(xcb, wx_ref[0]) + bx_ref[...])
    r_gate = _sigmoid(_dot(xcb, wa_ref[0]) + ba_ref[...])
    neg_l = -l_ref[...]
    softplus = jnp.maximum(neg_l, 0.0) + jnp.log1p(jnp.exp(-jnp.abs(neg_l)))
    log_a = (-LRU_C * softplus) * r_gate
    a = jnp.exp(log_a)
    a_s[...] = a
    om = 1.0 - a * a
    h_s[...] = jnp.where(om > 0.0, om * lax.rsqrt(om), 0.0) * (i_gate * xc)

    def step(t, h):
        r0 = pl.multiple_of(t * nb, nb)
        h = a_s[pl.ds(r0, nb), :] * h + h_s[pl.ds(r0, nb), :]
        h_s[pl.ds(r0, nb), :] = h
        return h

    carry_s[...] = lax.fori_loop(0, tm // nb, step, carry_s[...], unroll=4)
    y_ref[...] = (h_s[...] * _gelu_tanh(ga_ref[...])).astype(y_ref.dtype)


def _rg_lru(h_a, conv_w, conv_b, wx4, bx, wa4, ba, lam, nb, tm, halo_blk):
    t = h_a.shape[0]
    nc = LRU_WIDTH // LRU_CHUNK
    c = LRU_CHUNK
    row = lambda cc, i: (0, cc)
    return pl.pallas_call(
        functools.partial(_lru_kernel, nb=nb, tm=tm),
        grid=(nc, t // tm),
        in_specs=[
            pl.BlockSpec((tm, c), lambda cc, i: (i, cc)),
            pl.BlockSpec((halo_blk, c), lambda cc, i: (jnp.maximum(i * (tm // halo_blk) - 1, 0), cc)),
            pl.BlockSpec((tm, c), lambda cc, i: (i, nc + cc)),
            pl.BlockSpec((LRU_CONV, c), row),
            pl.BlockSpec((1, c), row),
            pl.BlockSpec((1, c, c), lambda cc, i: (cc, 0, 0)),
            pl.BlockSpec((1, c), row),
            pl.BlockSpec((1, c, c), lambda cc, i: (cc, 0, 0)),
            pl.BlockSpec((1, c), row),
            pl.BlockSpec((1, c), row),
        ],
        out_specs=pl.BlockSpec((tm, c), lambda cc, i: (i, cc)),
        out_shape=jax.ShapeDtypeStruct((t, LRU_WIDTH), BF16),
        scratch_shapes=[
            pltpu.VMEM((tm + (LRU_CONV - 1) * nb, c), F32),
            pltpu.VMEM((tm, c), F32),
            pltpu.VMEM((tm, c), F32),
            pltpu.VMEM((nb, c), F32),
        ],
        compiler_params=_params("parallel", "arbitrary"),
        name="rg_lru",
    )(h_a, h_a, h_a, conv_w, conv_b, wx4, bx, wa4, ba, lam)


def _paired_head_order(n_kvh, g):
    order = []
    for p in range(n_kvh // 2):
        for j in range(g):
            order += [(2 * p) * g + j, (2 * p + 1) * g + j]
    return order


def _head_columns(order):
    return jnp.concatenate([jnp.arange(h * HEAD_DIM, (h + 1) * HEAD_DIM) for h in order])


def _attn_kernel(*refs, g, tq, max_dist, has_sinks, has_lse):
    refs = list(refs)
    sink_ref = refs.pop(0) if has_sinks else None
    q_ref, kc_ref, vc_ref, kp_ref, vp_ref, o_ref = refs[:6]
    refs = refs[6:]
    lse_ref = refs.pop(0) if has_lse else None
    q_s, k_s, v_s, o_s = refs[:4]
    l_s = refs[4] if has_lse else None
    i = pl.program_id(0)
    pair = pl.program_id(2)
    ns = q_ref.shape[1]
    nblk = tq // BLOCK
    half = LANES // 2
    nrb = 2 * g

    q_s[...] = pltpu.einshape("tnw->ntw", q_ref[...])
    k_s[:, 0:BLOCK, :] = pltpu.einshape("tnw->ntw", kp_ref[...])
    k_s[:, BLOCK:, :] = pltpu.einshape("tnw->ntw", kc_ref[...])
    v_s[:, 0:BLOCK, :] = pltpu.einshape("tnw->ntw", vp_ref[...])
    v_s[:, BLOCK:, :] = pltpu.einshape("tnw->ntw", vc_ref[...])

    lo = lax.broadcasted_iota(jnp.int32, (1, LANES), 1) < half
    qi = lax.broadcasted_iota(jnp.int32, (BLOCK, 2 * BLOCK), 0)
    kj = lax.broadcasted_iota(jnp.int32, (BLOCK, 2 * BLOCK), 1)
    dist = qi + BLOCK - kj
    band = (dist >= 0) & (dist <= max_dist)
    bias_any = jnp.where(band, 0.0, MASK_BIAS).astype(F32)
    bias_first = jnp.where(band & (kj >= BLOCK), 0.0, MASK_BIAS).astype(F32)
    ones = jnp.ones((2 * BLOCK, LANES), BF16)
    if has_sinks:
        sinks = [sink_ref[(2 * pair + (rb % 2)) * g + rb // 2] for rb in range(nrb)]

    def seq_body(n, carry):
        for jb in range(nblk):
            qrows = slice(jb * BLOCK, (jb + 1) * BLOCK)
            krows = slice(jb * BLOCK, (jb + 2) * BLOCK)
            parts = []
            for j in range(g):
                qj = q_s[n, qrows, j * LANES:(j + 1) * LANES]
                zero = jnp.zeros_like(qj)
                parts += [jnp.where(lo, qj, zero), jnp.where(lo, zero, qj)]
            lhs = jnp.concatenate(parts, axis=0)
            s = lax.dot_general(lhs, k_s[n, krows, :], (((1,), (1,)), ((), ())),
                                preferred_element_type=F32)
            bias = jnp.where(i * nblk + jb > 0, bias_any, bias_first)
            m, p = [], []
            for rb in range(nrb):
                s_rb = s[rb * BLOCK:(rb + 1) * BLOCK, :] + bias
                m_rb = jnp.max(s_rb, axis=-1, keepdims=True)
                if has_sinks:
                    m_rb = jnp.maximum(m_rb, sinks[rb])
                m.append(m_rb)
                p.append(jnp.exp(s_rb - m_rb).astype(BF16))
            vones = jnp.concatenate([v_s[n, krows, :], ones], axis=1)
            od = _dot(jnp.concatenate(p, axis=0), vones)
            den = []
            for rb in range(nrb):
                d_rb = od[rb * BLOCK:(rb + 1) * BLOCK, LANES:2 * LANES]
                if has_sinks:
                    d_rb = d_rb + jnp.exp(sinks[rb] - m[rb])
                den.append(d_rb)
            for j in range(g):
                ra, rb = 2 * j, 2 * j + 1
                val = jnp.where(lo, od[ra * BLOCK:(ra + 1) * BLOCK, 0:LANES],
                                od[rb * BLOCK:(rb + 1) * BLOCK, 0:LANES])
                dsel = jnp.where(lo, den[ra], den[rb])
                o_s[n, qrows, j * LANES:(j + 1) * LANES] = (val / dsel).astype(o_s.dtype)
                if has_lse:
                    l_s[n, qrows, j * LANES:(j + 1) * LANES] = (
                        jnp.where(lo, m[ra], m[rb]) + jnp.log(dsel))
        return carry

    lax.fori_loop(0, ns, seq_body, 0, unroll=min(ns, 4 // nblk))
    o_ref[...] = pltpu.einshape("ntw->tnw", o_s[...])
    if has_lse:
        lse_ref[...] = pltpu.einshape("ntw->tnw", l_s[...])


def _banded_attention(q3, kv3, *, q_blk0, g, n_kvh, max_dist, tq, sinks=None, has_lse=False,
                      out_dtype=BF16):
    length, nseq, _ = q3.shape
    ns = min(ATTN_SEQS, nseq)
    npair = n_kvh // 2
    qbw = g * LANES
    rpb = tq // BLOCK
    cur = lambda off: (lambda i, n, p: (i, n, off + p))
    prev = lambda off: (lambda i, n, p: (jnp.maximum(i * rpb - 1, 0), n, off + p))
    in_specs = [
        pl.BlockSpec((tq, ns, qbw), cur(q_blk0)),
        pl.BlockSpec((tq, ns, LANES), cur(0)),
        pl.BlockSpec((tq, ns, LANES), cur(npair)),
        pl.BlockSpec((BLOCK, ns, LANES), prev(0)),
        pl.BlockSpec((BLOCK, ns, LANES), prev(npair)),
    ]
    args = [q3, kv3, kv3, kv3, kv3]
    if sinks is not None:
        in_specs.insert(0, pl.BlockSpec(memory_space=pltpu.SMEM))
        args.insert(0, sinks)
    ow = n_kvh * g * HEAD_DIM
    out_shape = [jax.ShapeDtypeStruct((length, nseq, ow), out_dtype)]
    out_specs = [pl.BlockSpec((tq, ns, qbw), cur(0))]
    scratch = [
        pltpu.VMEM((ns, tq, qbw), BF16),
        pltpu.VMEM((ns, tq + BLOCK, LANES), BF16),
        pltpu.VMEM((ns, tq + BLOCK, LANES), BF16),
        pltpu.VMEM((ns, tq, qbw), out_dtype),
    ]
    if has_lse:
        out_shape.append(jax.ShapeDtypeStruct((length, nseq, ow), F32))
        out_specs.append(pl.BlockSpec((tq, ns, qbw), cur(0)))
        scratch.append(pltpu.VMEM((ns, tq, qbw), F32))
    return pl.pallas_call(
        functools.partial(_attn_kernel, g=g, tq=tq, max_dist=max_dist,
                          has_sinks=sinks is not None, has_lse=has_lse),
        grid=(length // tq, nseq // ns, npair),
        in_specs=in_specs,
        out_specs=out_specs,
        out_shape=out_shape,
        scratch_shapes=scratch,
        compiler_params=_params("parallel", "parallel", "parallel"),
        name="banded_attn",
    )(*args)


def _s5_kernel(u_ref, bd_ref, lre_ref, lim_ref, cd_ref, d_ref, gw_ref, gb_ref, y_ref,
               sb_s, st_s, *, nb, tm):
    hw = S5_HALF

    @pl.when(pl.program_id(0) == 0)
    def _():
        st_s[...] = jnp.zeros_like(st_s)

    zs = []
    for j in range(S5_NBLK):
        u = u_ref[:, j * LANES:(j + 1) * LANES]
        sb_s[j] = _dot(u.astype(BF16), bd_ref[j])
        lre = jnp.broadcast_to(lre_ref[j], (nb, hw))
        lim = jnp.broadcast_to(lim_ref[j], (nb, hw))
        sre = st_s[j, :, 0:hw]
        sim = st_s[j, :, hw:2 * hw]
        for t in range(tm // nb):
            rows = slice(t * nb, (t + 1) * nb)
            sre, sim = (lre * sre - lim * sim + sb_s[j, rows, 0:hw],
                        lre * sim + lim * sre + sb_s[j, rows, hw:2 * hw])
            sb_s[j, rows, 0:hw] = sre
            sb_s[j, rows, hw:2 * hw] = sim
        st_s[j, :, 0:hw] = sre
        st_s[j, :, hw:2 * hw] = sim
        cs = _dot(sb_s[j].astype(BF16), cd_ref[j])
        y = cs[:, 0:LANES] + cs[:, LANES:2 * LANES] + d_ref[:, j * LANES:(j + 1) * LANES] * u
        zs.append(_gelu_tanh(y))
    z = jnp.concatenate(zs, axis=1)
    gate = _sigmoid(_dot(z.astype(BF16), gw_ref[...]) + gb_ref[...])
    y_ref[...] = (z * gate).astype(y_ref.dtype)


def _s5(u, bd, lre, lim, cd, dsk, glu_w, glu_b, nb, tm):
    t = u.shape[0]
    return pl.pallas_call(
        functools.partial(_s5_kernel, nb=nb, tm=tm),
        grid=(t // tm,),
        in_specs=[pl.BlockSpec((tm, S5_WIDTH), lambda i: (i, 0))]
        + [_resident(a.shape) for a in (bd, lre, lim, cd, dsk, glu_w, glu_b)],
        out_specs=pl.BlockSpec((tm, S5_WIDTH), lambda i: (i, 0)),
        out_shape=jax.ShapeDtypeStruct((t, S5_WIDTH), BF16),
        scratch_shapes=[
            pltpu.VMEM((S5_NBLK, tm, 2 * S5_HALF), F32),
            pltpu.VMEM((S5_NBLK, nb, 2 * S5_HALF), F32),
        ],
        compiler_params=_params("arbitrary"),
        name="s5",
    )(u, bd, lre, lim, cd, dsk, glu_w, glu_b)


def _outproj_even_kernel(ya_ref, ob_ref, x_ref, w_ref, g_ref, b_ref, o_ref, *, rows):
    k1 = ya_ref.shape[1]
    k2 = ob_ref.shape[1]
    for r0 in range(0, x_ref.shape[0], rows):
        rs = slice(r0, r0 + rows)
        mix = _dot(ya_ref[rs, :], w_ref[0:k1, :]) + _dot(ob_ref[rs, :], w_ref[k1:k1 + k2, :])
        o_ref[rs, :] = _layer_norm(ALPHA * x_ref[rs, :] + mix, g_ref[...], b_ref[...])


def _outproj_odd_kernel(yc_ref, o0_ref, o1_ref, o2_ref, l0_ref, l1_ref, l2_ref, x_ref, w_ref,
                        g_ref, b_ref, o_ref, *, rows):
    k1 = yc_ref.shape[1]
    k2 = o0_ref.shape[1]
    for r0 in range(0, x_ref.shape[0], rows):
        rs = slice(r0, r0 + rows)
        l0, l1, l2 = l0_ref[rs, :], l1_ref[rs, :], l2_ref[rs, :]
        m = jnp.maximum(jnp.maximum(l0, l1), l2)
        e0, e1, e2 = jnp.exp(l0 - m), jnp.exp(l1 - m), jnp.exp(l2 - m)
        yd = (e0 * o0_ref[rs, :] + e1 * o1_ref[rs, :] + e2 * o2_ref[rs, :]) / (e0 + e1 + e2)
        mix = _dot(yc_ref[rs, :], w_ref[0:k1, :]) + _dot(yd.astype(BF16), w_ref[k1:k1 + k2, :])
        o_ref[rs, :] = _layer_norm(ALPHA * x_ref[rs, :] + mix, g_ref[...], b_ref[...])


def _outproj_ln(kernel_fn, acts, x, w, ln_g, ln_b, tm, name):
    t, d = x.shape
    row = lambda i: (i, 0)
    return pl.pallas_call(
        functools.partial(kernel_fn, rows=256),
        grid=(t // tm,),
        in_specs=[pl.BlockSpec((tm, a.shape[1]), row) for a in acts]
        + [pl.BlockSpec((tm, d), row), _resident(w.shape), _resident((1, d)), _resident((1, d))],
        out_specs=pl.BlockSpec((tm, d), row),
        out_shape=jax.ShapeDtypeStruct((t, d), F32),
        compiler_params=_params("parallel"),
        name=name,
    )(*acts, x, w, ln_g, ln_b)


def _ffn_kernel(x_ref, *refs, nb, tm, nj, batch_major_out):
    per = FFN_TILES_PER_STEP
    tiles = [refs[5 * k:5 * k + 5] for k in range(per)]
    g_ref, b_ref, o_ref, xb_s, hg_s, hv_s, cg_s, cv_s, acc_s = refs[5 * per:]
    i = pl.program_id(0)
    j = pl.program_id(1)
    halo = (FFN_CONV - 1) * nb

    @pl.when(j == 0)
    def _():
        xb_s[...] = x_ref[...].astype(BF16)
        acc_s[...] = jnp.zeros_like(acc_s)

    def column_tile(jt, wg_ref, wv_ref, pg_ref, pv_ref, wd_ref):
        @pl.when(i == 0)
        def _():
            cg_s[jt] = jnp.zeros(cg_s.shape[1:], F32)
            cv_s[jt] = jnp.zeros(cv_s.shape[1:], F32)

        hg_s[0:halo, :] = cg_s[jt]
        hv_s[0:halo, :] = cv_s[jt]
        hg_s[halo:, :] = _dot(xb_s[...], wg_ref[...])
        hv_s[halo:, :] = _dot(xb_s[...], wv_ref[...])
        cg_s[jt] = hg_s[tm:tm + halo, :]
        cv_s[jt] = hv_s[tm:tm + halo, :]

        gate = pg_ref[FFN_CONV:FFN_CONV + 1, :]
        val = pv_ref[FFN_CONV:FFN_CONV + 1, :]
        for k in range(FFN_CONV):
            gate = gate + pg_ref[k:k + 1, :] * hg_s[k * nb:k * nb + tm, :]
            val = val + pv_ref[k:k + 1, :] * hv_s[k * nb:k * nb + tm, :]
        act = (gate * _sigmoid(gate) * val).astype(BF16)
        acc_s[...] += _dot(act, wd_ref[...])

    column_tile(per * j, *tiles[0])
    for k in range(1, per):
        @pl.when(per * j + k < nj)
        def _():
            column_tile(per * j + k, *tiles[k])

    @pl.when(j == pl.num_programs(1) - 1)
    def _():
        y = _layer_norm(ALPHA * x_ref[...] + acc_s[...], g_ref[...], b_ref[...])
        if batch_major_out:
            y = pltpu.einshape("sbd->bsd", y.reshape(tm // nb, nb, y.shape[1]))
        o_ref[...] = y


def _conv_ffn_ln(x, w_up, conv_w, conv_b, w_down, ln_g, ln_b, nb, tm, tf, batch_major_out):
    t, d = x.shape
    nj = D_FF // tf
    halo = (FFN_CONV - 1) * nb
    if batch_major_out:
        out_spec = pl.BlockSpec((nb, tm // nb, d), lambda i, j: (0, i, 0))
        out_shape = jax.ShapeDtypeStruct((nb, t // nb, d), F32)
    else:
        out_spec = pl.BlockSpec((tm, d), lambda i, j: (i, 0))
        out_shape = jax.ShapeDtypeStruct((t, d), F32)
    per = FFN_TILES_PER_STEP
    conv_p = jnp.concatenate([conv_w, conv_b], axis=0)
    tile_specs, tile_args = [], []
    for k in range(per):
        col = lambda j, k=k: jnp.minimum(per * j + k, nj - 1)
        tile_specs += [
            pl.BlockSpec((d, tf), lambda i, j, c=col: (0, c(j))),
            pl.BlockSpec((d, tf), lambda i, j, c=col: (0, nj + c(j))),
            pl.BlockSpec((FFN_CONV + 1, tf), lambda i, j, c=col: (0, c(j))),
            pl.BlockSpec((FFN_CONV + 1, tf), lambda i, j, c=col: (0, nj + c(j))),
            pl.BlockSpec((tf, d), lambda i, j, c=col: (c(j), 0)),
        ]
        tile_args += [w_up, w_up, conv_p, conv_p, w_down]
    return pl.pallas_call(
        functools.partial(_ffn_kernel, nb=nb, tm=tm, nj=nj, batch_major_out=batch_major_out),
        grid=(t // tm, pl.cdiv(nj, per)),
        in_specs=[pl.BlockSpec((tm, d), lambda i, j: (i, 0))] + tile_specs
        + [_resident(ln_g.shape), _resident(ln_b.shape)],
        out_specs=out_spec,
        out_shape=out_shape,
        scratch_shapes=[
            pltpu.VMEM((tm, d), BF16),
            pltpu.VMEM((tm + halo, tf), F32),
            pltpu.VMEM((tm + halo, tf), F32),
            pltpu.VMEM((nj, halo, tf), F32),
            pltpu.VMEM((nj, halo, tf), F32),
            pltpu.VMEM((tm, d), F32),
        ],
        compiler_params=_params("arbitrary", "arbitrary"),
        name="conv_ffn_ln",
    )(x, *tile_args, ln_g, ln_b)


def _row(v):
    return v.astype(F32).reshape(1, -1)


def _block_diag(w, per):
    n, c, d = w.shape
    eye = jnp.eye(per, dtype=w.dtype)
    out = jnp.einsum('qncd,nm->qncmd', w.reshape(n // per, per, c, d), eye)
    return out.reshape(n // per, per * c, per * d)


def _s5_matrices(a_re, a_im, log_dt, b_re, b_im, c_re, c_im):
    a_re, a_im = a_re.astype(F32), a_im.astype(F32)
    dt = jnp.exp(log_dt.astype(F32))[:, None]
    mag = jnp.exp(a_re * dt)
    lre = mag * jnp.cos(a_im * dt)
    lim = mag * jnp.sin(a_im * dt)
    den = a_re * a_re + a_im * a_im
    cre = ((lre - 1.0) * a_re + lim * a_im) / den
    cim = (lim * a_re - (lre - 1.0) * a_im) / den
    bb_re = cre[:, :, None] * b_re - cim[:, :, None] * b_im
    bb_im = cre[:, :, None] * b_im + cim[:, :, None] * b_re
    eye = jnp.eye(S5_GPB, dtype=F32)
    shp = (S5_NBLK, S5_GPB, S5_STATE, S5_GROUP)
    bd = jnp.concatenate([
        jnp.einsum('jgpc,gh->jgchp', m.reshape(shp), eye).reshape(S5_NBLK, LANES, S5_HALF)
        for m in (bb_re, bb_im)], axis=2)
    shc = (S5_NBLK, S5_GPB, S5_GROUP, S5_STATE)
    cd_re, cd_im = [
        jnp.einsum('jgcp,gh->jgphc', m.reshape(shc), eye).reshape(S5_NBLK, S5_HALF, LANES)
        for m in (c_re.astype(F32), -c_im.astype(F32))]
    zero = jnp.zeros_like(cd_re)
    cd = jnp.concatenate([jnp.concatenate([cd_re, zero], axis=2),
                          jnp.concatenate([zero, cd_im], axis=2)], axis=1)
    return (bd.astype(BF16), lre.reshape(S5_NBLK, 1, S5_HALF), lim.reshape(S5_NBLK, 1, S5_HALF),
            cd.astype(BF16))


def _even_layer(x, w_in, conv_w, conv_b, gx_w, gx_b, ga_w, ga_b, lru_l, sinks, w_out,
                ln1_g, ln1_b, ffn_up, ffn_conv_w, ffn_conv_b, ffn_down, ln2_g, ln2_b, *, tm):
    bsz, seq, d = x.shape
    t = bsz * seq
    g = SWA_HEADS // SWA_KV_HEADS
    qw = SWA_HEADS * HEAD_DIM
    kw = SWA_KV_HEADS * HEAD_DIM
    qcols = _head_columns(_paired_head_order(SWA_KV_HEADS, g))
    wf = w_in.astype(F32)
    o_q = 2 * LRU_WIDTH
    w_in_b = jnp.concatenate(
        [wf[:, :o_q], wf[:, o_q:o_q + qw][:, qcols] * HEAD_DIM ** -0.5, wf[:, o_q + qw:]],
        axis=1).astype(BF16)
    outs = ((2 * LRU_WIDTH, F32), (qw, BF16), (2 * kw, BF16))
    h_a, h_q, h_kv, x_tm = _inproj(x, w_in_b, outs, tm, True)

    per = LRU_CHUNK // (LRU_WIDTH // LRU_BLOCKS)
    ya = _rg_lru(h_a, conv_w.astype(F32), _row(conv_b),
                 _block_diag(gx_w.astype(F32), per).astype(BF16), _row(gx_b),
                 _block_diag(ga_w.astype(F32), per).astype(BF16), _row(ga_b), _row(lru_l),
                 bsz, min(2 * tm, t), 4 * bsz)

    ob, = _banded_attention(
        h_q.reshape(seq, bsz, qw), h_kv.reshape(seq, bsz, 2 * kw), q_blk0=0, g=g,
        n_kvh=SWA_KV_HEADS, max_dist=SWA_WINDOW - 1, tq=2 * BLOCK, sinks=sinks.astype(F32))

    w_out_b = jnp.concatenate([w_out[:LRU_WIDTH], w_out[LRU_WIDTH:][qcols]], axis=0).astype(BF16)
    x1 = _outproj_ln(_outproj_even_kernel, [ya, ob.reshape(t, qw)], x_tm, w_out_b,
                     _row(ln1_g), _row(ln1_b), tm, "outproj_even")
    return _conv_ffn_ln(x1, ffn_up.astype(BF16), ffn_conv_w.astype(F32), _row(ffn_conv_b),
                        ffn_down.astype(BF16), _row(ln2_g), _row(ln2_b), bsz, tm, 512, False)


def _odd_layer(x, bsz, w_in, a_re, a_im, log_dt, b_re, b_im, c_re, c_im, d_skip, glu_w, glu_b,
               w_out, ln1_g, ln1_b, ffn_up, ffn_conv_w, ffn_conv_b, ffn_down, ln2_g, ln2_b, *, tm):
    t, d = x.shape
    seq = t // bsz
    ncfg = len(DIL_CONFIGS)
    g = DIL_HEADS // DIL_KV_HEADS
    qw = DIL_HEADS * HEAD_DIM
    kw = DIL_KV_HEADS * HEAD_DIM
    qcols = _head_columns(_paired_head_order(DIL_KV_HEADS, g))
    wf = w_in.astype(F32)
    o_q = S5_WIDTH
    w_q = jnp.concatenate([wf[:, o_q + r * qw:o_q + (r + 1) * qw][:, qcols] for r in range(ncfg)],
                          axis=1) * HEAD_DIM ** -0.5
    w_in_b = jnp.concatenate([wf[:, :o_q], w_q, wf[:, o_q + ncfg * qw:]], axis=1).astype(BF16)
    outs = ((S5_WIDTH, F32), (ncfg * qw, BF16), (2 * kw, BF16))
    u, h_q, h_kv = _inproj(x, w_in_b, outs, tm, False)

    bd, lre, lim, cd = _s5_matrices(a_re, a_im, log_dt, b_re, b_im, c_re, c_im)
    yc = _s5(u, bd, lre, lim, cd, _row(d_skip), glu_w.astype(BF16), _row(glu_b), bsz, tm)

    o_l = []
    for r, (window, dil) in enumerate(DIL_CONFIGS):
        length = seq // dil
        nseq = dil * bsz
        o, lse = _banded_attention(
            h_q.reshape(length, nseq, ncfg * qw), h_kv.reshape(length, nseq, 2 * kw),
            q_blk0=r * (DIL_KV_HEADS // 2), g=g, n_kvh=DIL_KV_HEADS, max_dist=window // dil,
            tq=min(length, 2 * BLOCK), has_lse=True, out_dtype=F32)
        o_l.append((o.reshape(t, qw), lse.reshape(t, qw)))

    w_out_b = jnp.concatenate([w_out[:S5_WIDTH], w_out[S5_WIDTH:][qcols]], axis=0).astype(BF16)
    x1 = _outproj_ln(_outproj_odd_kernel, [yc] + [o for o, _ in o_l] + [l for _, l in o_l], x,
                     w_out_b, _row(ln1_g), _row(ln1_b), tm, "outproj_odd")
    return _conv_ffn_ln(x1, ffn_up.astype(BF16), ffn_conv_w.astype(F32), _row(ffn_conv_b),
                        ffn_down.astype(BF16), _row(ln2_g), _row(ln2_b), bsz, tm, 512, True)


def kernel(x, l0_w_in, l0_lru_conv_w, l0_lru_conv_b, l0_lru_gx_w, l0_lru_gx_b, l0_lru_ga_w, l0_lru_ga_b, l0_lru_L, l0_sinks, l0_w_out, l0_ln1_g, l0_ln1_b, l0_ffn_up, l0_ffn_conv_w, l0_ffn_conv_b, l0_ffn_down, l0_ln2_g, l0_ln2_b, l1_w_in, l1_s5_A_re, l1_s5_A_im, l1_s5_log_dt, l1_s5_B_re, l1_s5_B_im, l1_s5_C_re, l1_s5_C_im, l1_s5_D, l1_glu_w, l1_glu_b, l1_w_out, l1_ln1_g, l1_ln1_b, l1_ffn_up, l1_ffn_conv_w, l1_ffn_conv_b, l1_ffn_down, l1_ln2_g, l1_ln2_b):
    bsz, seq, d = x.shape
    assert seq % (DIL_CONFIGS[-1][1] * BLOCK) == 0 and bsz % 8 == 0
    tm = 32 * bsz
    h = _even_layer(x, l0_w_in, l0_lru_conv_w, l0_lru_conv_b, l0_lru_gx_w, l0_lru_gx_b, l0_lru_ga_w,
                    l0_lru_ga_b, l0_lru_L, l0_sinks, l0_w_out, l0_ln1_g, l0_ln1_b, l0_ffn_up,
                    l0_ffn_conv_w, l0_ffn_conv_b, l0_ffn_down, l0_ln2_g, l0_ln2_b, tm=tm)
    return _odd_layer(h, bsz, l1_w_in, l1_s5_A_re, l1_s5_A_im, l1_s5_log_dt, l1_s5_B_re, l1_s5_B_im,
                      l1_s5_C_re, l1_s5_C_im, l1_s5_D, l1_glu_w, l1_glu_b, l1_w_out, l1_ln1_g, l1_ln1_b,
                      l1_ffn_up, l1_ffn_conv_w, l1_ffn_conv_b, l1_ffn_down, l1_ln2_g, l1_ln2_b, tm=tm)
```

```python
import functools
import math

import jax
import jax.numpy as jnp
from jax import lax
from jax.experimental import pallas as pl
from jax.experimental.pallas import tpu as pltpu

F32 = jnp.float32
BF16 = jnp.bfloat16

HEAD_DIM = 64
BLOCK = 128
LANES = 128
LRU_WIDTH = 1024
LRU_BLOCKS = 16
LRU_CONV = 4
LRU_C = 8.0
LRU_CHUNK = 256
SWA_HEADS = 16
SWA_KV_HEADS = 4
SWA_WINDOW = 128
S5_WIDTH = 768
S5_GROUP = 16
S5_GROUPS = S5_WIDTH // S5_GROUP
S5_STATE = 64
S5_GPB = LANES // S5_GROUP
S5_NBLK = S5_WIDTH // LANES
S5_HALF = S5_GPB * S5_STATE
DIL_CONFIGS = ((128, 1), (512, 4), (2048, 16))
DIL_HEADS = 8
DIL_KV_HEADS = 4
D_FF = 5632
FFN_CONV = 3
FFN_COL_TILE = 512
FFN_TILES_PER_STEP = 2
ROW_TILE_STEPS = 32
LRU_HALO_STEPS = 4
ATTN_TQ = 2 * BLOCK
DEPTH = 2
ALPHA = (2 * DEPTH) ** 0.25
LN_EPS = 1e-5
MASK_BIAS = -1e30
ATTN_SEQS = 16

V7X_VMEM_LIMIT_BYTES = 56 * 1024 * 1024


def _params(*sem):
    return pltpu.CompilerParams(dimension_semantics=sem, vmem_limit_bytes=V7X_VMEM_LIMIT_BYTES)


def _resident(shape):
    nd = len(shape)
    return pl.BlockSpec(shape, lambda *_: (0,) * nd, pipeline_mode=pl.Buffered(1))


def _sigmoid(x):
    return 0.5 + 0.5 * jnp.tanh(0.5 * x)


def _gelu_tanh(x):
    return 0.5 * x * (1.0 + jnp.tanh(math.sqrt(2.0 / math.pi) * (x + 0.044715 * (x * x * x))))


def _layer_norm(y, g, b):
    mu = jnp.mean(y, axis=-1, keepdims=True)
    yc = y - mu
    var = jnp.mean(yc * yc, axis=-1, keepdims=True)
    return yc * lax.rsqrt(var + LN_EPS) * g + b


def _dot(a, b):
    return jnp.dot(a, b, preferred_element_type=F32)


def _inproj_kernel(x_ref, *refs, nw, chunk, batch_major_in):
    w_refs, o_refs = refs[:nw], refs[nw:]
    if batch_major_in:
        nb, ts, d = x_ref.shape
        x = pltpu.einshape("bsd->sbd", x_ref[...]).reshape(ts * nb, d)
        o_refs[-1][...] = x
    else:
        x = x_ref[...]
    xb = x.astype(BF16)
    for w_ref, o_ref in zip(w_refs, o_refs):
        width = o_ref.shape[1]
        for c0 in range(0, width, chunk):
            c1 = min(c0 + chunk, width)
            o_ref[:, c0:c1] = _dot(xb, w_ref[:, c0:c1]).astype(o_ref.dtype)


def _inproj(x, ws, dtypes, tm, batch_major_in):
    outs = [(w.shape[1], dt) for w, dt in zip(ws, dtypes)]
    if batch_major_in:
        nb, seq, d = x.shape
        t = nb * seq
        x_spec = pl.BlockSpec((nb, tm // nb, d), lambda i: (0, i, 0))
        outs.append((d, F32))
    else:
        t, d = x.shape
        x_spec = pl.BlockSpec((tm, d), lambda i: (i, 0))
    return pl.pallas_call(
        functools.partial(_inproj_kernel, nw=len(ws), chunk=512, batch_major_in=batch_major_in),
        grid=(t // tm,),
        in_specs=[x_spec] + [_resident(w.shape) for w in ws],
        out_specs=[pl.BlockSpec((tm, wd), lambda i: (i, 0)) for wd, _ in outs],
        out_shape=[jax.ShapeDtypeStruct((t, wd), dt) for wd, dt in outs],
        compiler_params=_params("parallel"),
        name="inproj",
    )(x, *ws)


def _lru_kernel(xa_ref, halo_ref, ga_ref, cw_ref, cb_ref, wx_ref, bx_ref, wa_ref, ba_ref, l_ref,
                y_ref, ext_s, a_s, h_s, carry_s, *, nb, tm):
    i = pl.program_id(1)
    halo_rows = (LRU_CONV - 1) * nb

    @pl.when(i == 0)
    def _():
        carry_s[...] = jnp.zeros_like(carry_s)

    halo = halo_ref[halo_ref.shape[0] - halo_rows:, :]
    ext_s[0:halo_rows, :] = jnp.where(i > 0, halo, 0.0)
    ext_s[halo_rows:, :] = xa_ref[...]
    xc = cb_ref[...]
    for k in range(LRU_CONV):
        xc = xc + cw_ref[k:k + 1, :] * ext_s[k * nb:k * nb + tm, :]

    xcb = xc.astype(BF16)
    i_gate = _sigmoid(_dot(xcb, wx_ref[0]) + bx_ref[...])
    r_gate = _sigmoid(_dot(xcb, wa_ref[0]) + ba_ref[...])
    neg_l = -l_ref[...]
    softplus = jnp.maximum(neg_l, 0.0) + jnp.log1p(jnp.exp(-jnp.abs(neg_l)))
    log_a = (-LRU_C * softplus) * r_gate
    a = jnp.exp(log_a)
    a_s[...] = a
    om = 1.0 - a * a
    h_s[...] = jnp.where(om > 0.0, om * lax.rsqrt(om), 0.0) * (i_gate * xc)

    def step(t, h):
        r0 = pl.multiple_of(t * nb, nb)
        h = a_s[pl.ds(r0, nb), :] * h + h_s[pl.ds(r0, nb), :]
        h_s[pl.ds(r0, nb), :] = h
        return h

    carry_s[...] = lax.fori_loop(0, tm // nb, step, carry_s[...], unroll=4)
    y_ref[...] = (h_s[...] * _gelu_tanh(ga_ref[...])).astype(y_ref.dtype)


def _rg_lru(h_a, conv_w, conv_b, wx4, bx, wa4, ba, lam, nb, tm, halo_blk):
    t = h_a.shape[0]
    nc = LRU_WIDTH // LRU_CHUNK
    c = LRU_CHUNK
    row = lambda cc, i: (0, cc)
    return pl.pallas_call(
        functools.partial(_lru_kernel, nb=nb, tm=tm),
        grid=(nc, t // tm),
        in_specs=[
            pl.BlockSpec((tm, c), lambda cc, i: (i, cc)),
            pl.BlockSpec((halo_blk, c), lambda cc, i: (jnp.maximum(i * (tm // halo_blk) - 1, 0), cc)),
            pl.BlockSpec((tm, c), lambda cc, i: (i, nc + cc)),
            pl.BlockSpec((LRU_CONV, c), row),
            pl.BlockSpec((1, c), row),
            pl.BlockSpec((1, c, c), lambda cc, i: (cc, 0, 0)),
            pl.BlockSpec((1, c), row),
            pl.BlockSpec((1, c, c), lambda cc, i: (cc, 0, 0)),
            pl.BlockSpec((1, c), row),
            pl.BlockSpec((1, c), row),
        ],
        out_specs=pl.BlockSpec((tm, c), lambda cc, i: (i, cc)),
        out_shape=jax.ShapeDtypeStruct((t, LRU_WIDTH), BF16),
        scratch_shapes=[
            pltpu.VMEM((tm + (LRU_CONV - 1) * nb, c), F32),
            pltpu.VMEM((tm, c), F32),
            pltpu.VMEM((tm, c), F32),
            pltpu.VMEM((nb, c), F32),
        ],
        compiler_params=_params("parallel", "arbitrary"),
        name="rg_lru",
    )(h_a, h_a, h_a, conv_w, conv_b, wx4, bx, wa4, ba, lam)


def _pair_heads(w, n_kvh, g, axis):
    w = jnp.moveaxis(w, axis, -1)
    lead = w.shape[:-1]
    w = w.reshape(lead + (-1, n_kvh // 2, 2, g, HEAD_DIM))
    w = jnp.swapaxes(w, -3, -2)
    return jnp.moveaxis(w.reshape(lead + (-1,)), -1, axis)


def _attn_kernel(*refs, g, tq, max_dist, has_sinks, has_lse):
    refs = list(refs)
    sink_ref = refs.pop(0) if has_sinks else None
    q_ref, kc_ref, vc_ref, kp_ref, vp_ref, o_ref = refs[:6]
    refs = refs[6:]
    lse_ref = refs.pop(0) if has_lse else None
    q_s, k_s, v_s, o_s = refs[:4]
    l_s = refs[4] if has_lse else None
    i = pl.program_id(0)
    pair = pl.program_id(2)
    ns = q_ref.shape[1]
    nblk = tq // BLOCK
    half = LANES // 2
    nrb = 2 * g

    q_s[...] = pltpu.einshape("tnw->ntw", q_ref[...])
    k_s[:, 0:BLOCK, :] = pltpu.einshape("tnw->ntw", kp_ref[...])
    k_s[:, BLOCK:, :] = pltpu.einshape("tnw->ntw", kc_ref[...])
    v_s[:, 0:BLOCK, :] = pltpu.einshape("tnw->ntw", vp_ref[...])
    v_s[:, BLOCK:, :] = pltpu.einshape("tnw->ntw", vc_ref[...])

    lo = lax.broadcasted_iota(jnp.int32, (1, LANES), 1) < half
    qi = lax.broadcasted_iota(jnp.int32, (BLOCK, 2 * BLOCK), 0)
    kj = lax.broadcasted_iota(jnp.int32, (BLOCK, 2 * BLOCK), 1)
    dist = qi + BLOCK - kj
    band = (dist >= 0) & (dist <= max_dist)
    bias_any = jnp.where(band, 0.0, MASK_BIAS).astype(F32)
    bias_first = jnp.where(band & (kj >= BLOCK), 0.0, MASK_BIAS).astype(F32)
    ones = jnp.ones((2 * BLOCK, LANES), BF16)
    if has_sinks:
        sinks = [sink_ref[(2 * pair + (rb % 2)) * g + rb // 2] for rb in range(nrb)]

    def seq_body(n, carry):
        for jb in range(nblk):
            qrows = slice(jb * BLOCK, (jb + 1) * BLOCK)
            krows = slice(jb * BLOCK, (jb + 2) * BLOCK)
            parts = []
            for j in range(g):
                qj = q_s[n, qrows, j * LANES:(j + 1) * LANES]
                zero = jnp.zeros_like(qj)
                parts += [jnp.where(lo, qj, zero), jnp.where(lo, zero, qj)]
            lhs = jnp.concatenate(parts, axis=0)
            s = lax.dot_general(lhs, k_s[n, krows, :], (((1,), (1,)), ((), ())),
                                preferred_element_type=F32)
            bias = jnp.where(i * nblk + jb > 0, bias_any, bias_first)
            m, p = [], []
            for rb in range(nrb):
                s_rb = s[rb * BLOCK:(rb + 1) * BLOCK, :] + bias
                m_rb = jnp.max(s_rb, axis=-1, keepdims=True)
                if has_sinks:
                    m_rb = jnp.maximum(m_rb, sinks[rb])
                m.append(m_rb)
                p.append(jnp.exp(s_rb - m_rb).astype(BF16))
            vones = jnp.concatenate([v_s[n, krows, :], ones], axis=1)
            od = _dot(jnp.concatenate(p, axis=0), vones)
            den = []
            for rb in range(nrb):
                d_rb = od[rb * BLOCK:(rb + 1) * BLOCK, LANES:2 * LANES]
                if has_sinks:
                    d_rb = d_rb + jnp.exp(sinks[rb] - m[rb])
                den.append(d_rb)
            for j in range(g):
                ra, rb = 2 * j, 2 * j + 1
                val = jnp.where(lo, od[ra * BLOCK:(ra + 1) * BLOCK, 0:LANES],
                                od[rb * BLOCK:(rb + 1) * BLOCK, 0:LANES])
                dsel = jnp.where(lo, den[ra], den[rb])
                o_s[n, qrows, j * LANES:(j + 1) * LANES] = (val / dsel).astype(o_s.dtype)
                if has_lse:
                    l_s[n, qrows, j * LANES:(j + 1) * LANES] = (
                        jnp.where(lo, m[ra], m[rb]) + jnp.log(dsel))
        return carry

    lax.fori_loop(0, ns, seq_body, 0, unroll=min(ns, 4 // nblk))
    o_ref[...] = pltpu.einshape("ntw->tnw", o_s[...])
    if has_lse:
        lse_ref[...] = pltpu.einshape("ntw->tnw", l_s[...])


def _banded_attention(q3, kv3, *, q_blk0, g, n_kvh, max_dist, tq, sinks=None, has_lse=False,
                      out_dtype=BF16):
    length, nseq, _ = q3.shape
    ns = min(ATTN_SEQS, nseq)
    npair = n_kvh // 2
    qbw = g * LANES
    rpb = tq // BLOCK
    cur = lambda off: (lambda i, n, p: (i, n, off + p))
    prev = lambda off: (lambda i, n, p: (jnp.maximum(i * rpb - 1, 0), n, off + p))
    in_specs = [
        pl.BlockSpec((tq, ns, qbw), cur(q_blk0)),
        pl.BlockSpec((tq, ns, LANES), cur(0)),
        pl.BlockSpec((tq, ns, LANES), cur(npair)),
        pl.BlockSpec((BLOCK, ns, LANES), prev(0)),
        pl.BlockSpec((BLOCK, ns, LANES), prev(npair)),
    ]
    args = [q3, kv3, kv3, kv3, kv3]
    if sinks is not None:
        in_specs.insert(0, pl.BlockSpec(memory_space=pltpu.SMEM))
        args.insert(0, sinks)
    ow = n_kvh * g * HEAD_DIM
    out_shape = [jax.ShapeDtypeStruct((length, nseq, ow), out_dtype)]
    out_specs = [pl.BlockSpec((tq, ns, qbw), cur(0))]
    scratch = [
        pltpu.VMEM((ns, tq, qbw), BF16),
        pltpu.VMEM((ns, tq + BLOCK, LANES), BF16),
        pltpu.VMEM((ns, tq + BLOCK, LANES), BF16),
        pltpu.VMEM((ns, tq, qbw), out_dtype),
    ]
    if has_lse:
        out_shape.append(jax.ShapeDtypeStruct((length, nseq, ow), F32))
        out_specs.append(pl.BlockSpec((tq, ns, qbw), cur(0)))
        scratch.append(pltpu.VMEM((ns, tq, qbw), F32))
    return pl.pallas_call(
        functools.partial(_attn_kernel, g=g, tq=tq, max_dist=max_dist,
                          has_sinks=sinks is not None, has_lse=has_lse),
        grid=(length // tq, nseq // ns, npair),
        in_specs=in_specs,
        out_specs=out_specs,
        out_shape=out_shape,
        scratch_shapes=scratch,
        compiler_params=_params("parallel", "parallel", "parallel"),
        name="banded_attn",
    )(*args)


def _s5_kernel(u_ref, bd_ref, lre_ref, lim_ref, cd_ref, d_ref, gw_ref, gb_ref, y_ref,
               sb_s, st_s, *, nb, tm):
    hw = S5_HALF

    @pl.when(pl.program_id(0) == 0)
    def _():
        st_s[...] = jnp.zeros_like(st_s)

    zs = []
    for j in range(S5_NBLK):
        u = u_ref[:, j * LANES:(j + 1) * LANES]
        sb_s[j] = _dot(u.astype(BF16), bd_ref[j])
        lre = jnp.broadcast_to(lre_ref[j], (nb, hw))
        lim = jnp.broadcast_to(lim_ref[j], (nb, hw))
        sre = st_s[j, :, 0:hw]
        sim = st_s[j, :, hw:2 * hw]
        for t in range(tm // nb):
            rows = slice(t * nb, (t + 1) * nb)
            sre, sim = (lre * sre - lim * sim + sb_s[j, rows, 0:hw],
                        lre * sim + lim * sre + sb_s[j, rows, hw:2 * hw])
            sb_s[j, rows, 0:hw] = sre
            sb_s[j, rows, hw:2 * hw] = sim
        st_s[j, :, 0:hw] = sre
        st_s[j, :, hw:2 * hw] = sim
        cs = _dot(sb_s[j].astype(BF16), cd_ref[j])
        y = cs[:, 0:LANES] + cs[:, LANES:2 * LANES] + d_ref[:, j * LANES:(j + 1) * LANES] * u
        zs.append(_gelu_tanh(y))
    z = jnp.concatenate(zs, axis=1)
    gate = _sigmoid(_dot(z.astype(BF16), gw_ref[...]) + gb_ref[...])
    y_ref[...] = (z * gate).astype(y_ref.dtype)


def _s5(u, bd, lre, lim, cd, dsk, glu_w, glu_b, nb, tm):
    t = u.shape[0]
    return pl.pallas_call(
        functools.partial(_s5_kernel, nb=nb, tm=tm),
        grid=(t // tm,),
        in_specs=[pl.BlockSpec((tm, S5_WIDTH), lambda i: (i, 0))]
        + [_resident(a.shape) for a in (bd, lre, lim, cd, dsk, glu_w, glu_b)],
        out_specs=pl.BlockSpec((tm, S5_WIDTH), lambda i: (i, 0)),
        out_shape=jax.ShapeDtypeStruct((t, S5_WIDTH), BF16),
        scratch_shapes=[
            pltpu.VMEM((S5_NBLK, tm, 2 * S5_HALF), F32),
            pltpu.VMEM((S5_NBLK, nb, 2 * S5_HALF), F32),
        ],
        compiler_params=_params("arbitrary"),
        name="s5",
    )(u, bd, lre, lim, cd, dsk, glu_w, glu_b)


def _outproj_even_kernel(ya_ref, ob_ref, x_ref, w1_ref, w2_ref, g_ref, b_ref, o_ref, *, rows):
    for r0 in range(0, x_ref.shape[0], rows):
        rs = slice(r0, r0 + rows)
        mix = _dot(ya_ref[rs, :], w1_ref[...]) + _dot(ob_ref[rs, :], w2_ref[...])
        o_ref[rs, :] = _layer_norm(ALPHA * x_ref[rs, :] + mix, g_ref[...], b_ref[...])


def _outproj_odd_kernel(yc_ref, o0_ref, o1_ref, o2_ref, l0_ref, l1_ref, l2_ref, x_ref, w1_ref,
                        w2_ref, g_ref, b_ref, o_ref, *, rows):
    for r0 in range(0, x_ref.shape[0], rows):
        rs = slice(r0, r0 + rows)
        l0, l1, l2 = l0_ref[rs, :], l1_ref[rs, :], l2_ref[rs, :]
        m = jnp.maximum(jnp.maximum(l0, l1), l2)
        e0, e1, e2 = jnp.exp(l0 - m), jnp.exp(l1 - m), jnp.exp(l2 - m)
        yd = (e0 * o0_ref[rs, :] + e1 * o1_ref[rs, :] + e2 * o2_ref[rs, :]) / (e0 + e1 + e2)
        mix = _dot(yc_ref[rs, :], w1_ref[...]) + _dot(yd.astype(BF16), w2_ref[...])
        o_ref[rs, :] = _layer_norm(ALPHA * x_ref[rs, :] + mix, g_ref[...], b_ref[...])


def _outproj_ln(kernel_fn, acts, x, w1, w2, ln_g, ln_b, tm, name):
    t, d = x.shape
    row = lambda i: (i, 0)
    return pl.pallas_call(
        functools.partial(kernel_fn, rows=256),
        grid=(t // tm,),
        in_specs=[pl.BlockSpec((tm, a.shape[1]), row) for a in acts]
        + [pl.BlockSpec((tm, d), row), _resident(w1.shape), _resident(w2.shape),
           _resident((1, d)), _resident((1, d))],
        out_specs=pl.BlockSpec((tm, d), row),
        out_shape=jax.ShapeDtypeStruct((t, d), F32),
        compiler_params=_params("parallel"),
        name=name,
    )(*acts, x, w1, w2, ln_g, ln_b)


def _ffn_kernel(x_ref, *refs, nb, tm, nj, batch_major_out):
    per = FFN_TILES_PER_STEP
    tiles = [refs[5 * k:5 * k + 5] for k in range(per)]
    g_ref, b_ref, o_ref, xb_s, hg_s, hv_s, cg_s, cv_s, acc_s = refs[5 * per:]
    i = pl.program_id(0)
    j = pl.program_id(1)
    halo = (FFN_CONV - 1) * nb

    @pl.when(j == 0)
    def _():
        xb_s[...] = x_ref[...].astype(BF16)
        acc_s[...] = jnp.zeros_like(acc_s)

    def column_tile(jt, wg_ref, wv_ref, pg_ref, pv_ref, wd_ref):
        @pl.when(i == 0)
        def _():
            cg_s[jt] = jnp.zeros(cg_s.shape[1:], F32)
            cv_s[jt] = jnp.zeros(cv_s.shape[1:], F32)

        hg_s[0:halo, :] = cg_s[jt]
        hv_s[0:halo, :] = cv_s[jt]
        hg_s[halo:, :] = _dot(xb_s[...], wg_ref[...])
        hv_s[halo:, :] = _dot(xb_s[...], wv_ref[...])
        cg_s[jt] = hg_s[tm:tm + halo, :]
        cv_s[jt] = hv_s[tm:tm + halo, :]

        gate = pg_ref[FFN_CONV:FFN_CONV + 1, :]
        val = pv_ref[FFN_CONV:FFN_CONV + 1, :]
        for k in range(FFN_CONV):
            gate = gate + pg_ref[k:k + 1, :] * hg_s[k * nb:k * nb + tm, :]
            val = val + pv_ref[k:k + 1, :] * hv_s[k * nb:k * nb + tm, :]
        act = (gate * _sigmoid(gate) * val).astype(BF16)
        acc_s[...] += _dot(act, wd_ref[...])

    column_tile(per * j, *tiles[0])
    for k in range(1, per):
        @pl.when(per * j + k < nj)
        def _():
            column_tile(per * j + k, *tiles[k])

    @pl.when(j == pl.num_programs(1) - 1)
    def _():
        y = _layer_norm(ALPHA * x_ref[...] + acc_s[...], g_ref[...], b_ref[...])
        if batch_major_out:
            y = pltpu.einshape("sbd->bsd", y.reshape(tm // nb, nb, y.shape[1]))
        o_ref[...] = y


def _conv_ffn_ln(x, w_up, conv_w, conv_b, w_down, ln_g, ln_b, nb, tm, tf, batch_major_out):
    t, d = x.shape
    nj = D_FF // tf
    halo = (FFN_CONV - 1) * nb
    if batch_major_out:
        out_spec = pl.BlockSpec((nb, tm // nb, d), lambda i, j: (0, i, 0))
        out_shape = jax.ShapeDtypeStruct((nb, t // nb, d), F32)
    else:
        out_spec = pl.BlockSpec((tm, d), lambda i, j: (i, 0))
        out_shape = jax.ShapeDtypeStruct((t, d), F32)
    per = FFN_TILES_PER_STEP
    conv_p = jnp.concatenate([conv_w, conv_b], axis=0)
    tile_specs, tile_args = [], []
    for k in range(per):
        col = lambda j, k=k: jnp.minimum(per * j + k, nj - 1)
        tile_specs += [
            pl.BlockSpec((d, tf), lambda i, j, c=col: (0, c(j))),
            pl.BlockSpec((d, tf), lambda i, j, c=col: (0, nj + c(j))),
            pl.BlockSpec((FFN_CONV + 1, tf), lambda i, j, c=col: (0, c(j))),
            pl.BlockSpec((FFN_CONV + 1, tf), lambda i, j, c=col: (0, nj + c(j))),
            pl.BlockSpec((tf, d), lambda i, j, c=col: (c(j), 0)),
        ]
        tile_args += [w_up, w_up, conv_p, conv_p, w_down]
    return pl.pallas_call(
        functools.partial(_ffn_kernel, nb=nb, tm=tm, nj=nj, batch_major_out=batch_major_out),
        grid=(t // tm, pl.cdiv(nj, per)),
        in_specs=[pl.BlockSpec((tm, d), lambda i, j: (i, 0))] + tile_specs
        + [_resident(ln_g.shape), _resident(ln_b.shape)],
        out_specs=out_spec,
        out_shape=out_shape,
        scratch_shapes=[
            pltpu.VMEM((tm, d), BF16),
            pltpu.VMEM((tm + halo, tf), F32),
            pltpu.VMEM((tm + halo, tf), F32),
            pltpu.VMEM((nj, halo, tf), F32),
            pltpu.VMEM((nj, halo, tf), F32),
            pltpu.VMEM((tm, d), F32),
        ],
        compiler_params=_params("arbitrary", "arbitrary"),
        name="conv_ffn_ln",
    )(x, *tile_args, ln_g, ln_b)


def _row(v):
    return v.astype(F32).reshape(1, -1)


def _block_diag(w, per):
    n, c, d = w.shape
    eye = jnp.eye(per, dtype=w.dtype)
    out = jnp.einsum('qncd,nm->qncmd', w.reshape(n // per, per, c, d), eye)
    return out.reshape(n // per, per * c, per * d)


def _s5_matrices(a_re, a_im, log_dt, b_re, b_im, c_re, c_im):
    a_re, a_im = a_re.astype(F32), a_im.astype(F32)
    dt = jnp.exp(log_dt.astype(F32))[:, None]
    mag = jnp.exp(a_re * dt)
    lre = mag * jnp.cos(a_im * dt)
    lim = mag * jnp.sin(a_im * dt)
    den = a_re * a_re + a_im * a_im
    cre = ((lre - 1.0) * a_re + lim * a_im) / den
    cim = (lim * a_re - (lre - 1.0) * a_im) / den
    bb_re = cre[:, :, None] * b_re - cim[:, :, None] * b_im
    bb_im = cre[:, :, None] * b_im + cim[:, :, None] * b_re
    eye = jnp.eye(S5_GPB, dtype=F32)
    shp = (S5_NBLK, S5_GPB, S5_STATE, S5_GROUP)
    bd = jnp.concatenate([
        jnp.einsum('jgpc,gh->jgchp', m.reshape(shp), eye).reshape(S5_NBLK, LANES, S5_HALF)
        for m in (bb_re, bb_im)], axis=2)
    shc = (S5_NBLK, S5_GPB, S5_GROUP, S5_STATE)
    cd_re, cd_im = [
        jnp.einsum('jgcp,gh->jgphc', m.reshape(shc), eye).reshape(S5_NBLK, S5_HALF, LANES)
        for m in (c_re.astype(F32), -c_im.astype(F32))]
    zero = jnp.zeros_like(cd_re)
    cd = jnp.concatenate([jnp.concatenate([cd_re, zero], axis=2),
                          jnp.concatenate([zero, cd_im], axis=2)], axis=1)
    return (bd.astype(BF16), lre.reshape(S5_NBLK, 1, S5_HALF), lim.reshape(S5_NBLK, 1, S5_HALF),
            cd.astype(BF16))


def _even_layer(x, w_in, conv_w, conv_b, gx_w, gx_b, ga_w, ga_b, lru_l, sinks, w_out,
                ln1_g, ln1_b, ffn_up, ffn_conv_w, ffn_conv_b, ffn_down, ln2_g, ln2_b, *, tm):
    bsz, seq, d = x.shape
    t = bsz * seq
    g = SWA_HEADS // SWA_KV_HEADS
    qw = SWA_HEADS * HEAD_DIM
    kw = SWA_KV_HEADS * HEAD_DIM
    o_q = 2 * LRU_WIDTH
    w_q = _pair_heads(w_in[:, o_q:o_q + qw], SWA_KV_HEADS, g, 1) * HEAD_DIM ** -0.5
    ws = [w.astype(BF16) for w in (w_in[:, :o_q], w_q, w_in[:, o_q + qw:])]
    h_a, h_q, h_kv, x_tm = _inproj(x, ws, (F32, BF16, BF16), tm, True)

    per = LRU_CHUNK // (LRU_WIDTH // LRU_BLOCKS)
    ya = _rg_lru(h_a, conv_w.astype(F32), _row(conv_b),
                 _block_diag(gx_w.astype(F32), per).astype(BF16), _row(gx_b),
                 _block_diag(ga_w.astype(F32), per).astype(BF16), _row(ga_b), _row(lru_l),
                 bsz, min(2 * tm, t), LRU_HALO_STEPS * bsz)

    ob, = _banded_attention(
        h_q.reshape(seq, bsz, qw), h_kv.reshape(seq, bsz, 2 * kw), q_blk0=0, g=g,
        n_kvh=SWA_KV_HEADS, max_dist=SWA_WINDOW - 1, tq=ATTN_TQ, sinks=sinks.astype(F32))

    x1 = _outproj_ln(_outproj_even_kernel, [ya, ob.reshape(t, qw)], x_tm,
                     w_out[:LRU_WIDTH].astype(BF16),
                     _pair_heads(w_out[LRU_WIDTH:], SWA_KV_HEADS, g, 0).astype(BF16),
                     _row(ln1_g), _row(ln1_b), tm, "outproj_even")
    return _conv_ffn_ln(x1, ffn_up.astype(BF16), ffn_conv_w.astype(F32), _row(ffn_conv_b),
                        ffn_down.astype(BF16), _row(ln2_g), _row(ln2_b), bsz, tm, FFN_COL_TILE, False)


def _odd_layer(x, bsz, w_in, a_re, a_im, log_dt, b_re, b_im, c_re, c_im, d_skip, glu_w, glu_b,
               w_out, ln1_g, ln1_b, ffn_up, ffn_conv_w, ffn_conv_b, ffn_down, ln2_g, ln2_b, *, tm):
    t, d = x.shape
    seq = t // bsz
    ncfg = len(DIL_CONFIGS)
    g = DIL_HEADS // DIL_KV_HEADS
    qw = DIL_HEADS * HEAD_DIM
    kw = DIL_KV_HEADS * HEAD_DIM
    o_q = S5_WIDTH
    w_q = _pair_heads(w_in[:, o_q:o_q + ncfg * qw], DIL_KV_HEADS, g, 1) * HEAD_DIM ** -0.5
    ws = [w.astype(BF16) for w in (w_in[:, :o_q], w_q, w_in[:, o_q + ncfg * qw:])]
    u, h_q, h_kv = _inproj(x, ws, (F32, BF16, BF16), tm, False)

    bd, lre, lim, cd = _s5_matrices(a_re, a_im, log_dt, b_re, b_im, c_re, c_im)
    yc = _s5(u, bd, lre, lim, cd, _row(d_skip), glu_w.astype(BF16), _row(glu_b), bsz, tm)

    o_l = []
    for r, (window, dil) in enumerate(DIL_CONFIGS):
        length = seq // dil
        nseq = dil * bsz
        o, lse = _banded_attention(
            h_q.reshape(length, nseq, ncfg * qw), h_kv.reshape(length, nseq, 2 * kw),
            q_blk0=r * (DIL_KV_HEADS // 2), g=g, n_kvh=DIL_KV_HEADS, max_dist=window // dil,
            tq=min(length, ATTN_TQ), has_lse=True, out_dtype=F32)
        o_l.append((o.reshape(t, qw), lse.reshape(t, qw)))

    x1 = _outproj_ln(_outproj_odd_kernel, [yc] + [o for o, _ in o_l] + [l for _, l in o_l], x,
                     w_out[:S5_WIDTH].astype(BF16),
                     _pair_heads(w_out[S5_WIDTH:], DIL_KV_HEADS, g, 0).astype(BF16),
                     _row(ln1_g), _row(ln1_b), tm, "outproj_odd")
    return _conv_ffn_ln(x1, ffn_up.astype(BF16), ffn_conv_w.astype(F32), _row(ffn_conv_b),
                        ffn_down.astype(BF16), _row(ln2_g), _row(ln2_b), bsz, tm, FFN_COL_TILE, True)


def kernel(x, l0_w_in, l0_lru_conv_w, l0_lru_conv_b, l0_lru_gx_w, l0_lru_gx_b, l0_lru_ga_w, l0_lru_ga_b, l0_lru_L, l0_sinks, l0_w_out, l0_ln1_g, l0_ln1_b, l0_ffn_up, l0_ffn_conv_w, l0_ffn_conv_b, l0_ffn_down, l0_ln2_g, l0_ln2_b, l1_w_in, l1_s5_A_re, l1_s5_A_im, l1_s5_log_dt, l1_s5_B_re, l1_s5_B_im, l1_s5_C_re, l1_s5_C_im, l1_s5_D, l1_glu_w, l1_glu_b, l1_w_out, l1_ln1_g, l1_ln1_b, l1_ffn_up, l1_ffn_conv_w, l1_ffn_conv_b, l1_ffn_down, l1_ln2_g, l1_ln2_b):
    bsz, seq, d = x.shape
    assert seq % (DIL_CONFIGS[-1][1] * BLOCK) == 0 and bsz % 8 == 0
    tm = ROW_TILE_STEPS * bsz
    h = _even_layer(x, l0_w_in, l0_lru_conv_w, l0_lru_conv_b, l0_lru_gx_w, l0_lru_gx_b, l0_lru_ga_w,
                    l0_lru_ga_b, l0_lru_L, l0_sinks, l0_w_out, l0_ln1_g, l0_ln1_b, l0_ffn_up,
                    l0_ffn_conv_w, l0_ffn_conv_b, l0_ffn_down, l0_ln2_g, l0_ln2_b, tm=tm)
    return _odd_layer(h, bsz, l1_w_in, l1_s5_A_re, l1_s5_A_im, l1_s5_log_dt, l1_s5_B_re, l1_s5_B_im,
                      l1_s5_C_re, l1_s5_C_im, l1_s5_D, l1_glu_w, l1_glu_b, l1_w_out, l1_ln1_g, l1_ln1_b,
                      l1_ffn_up, l1_ffn_conv_w, l1_ffn_conv_b, l1_ffn_down, l1_ln2_g, l1_ln2_b, tm=tm)
```

```python
import functools
import math

import jax
import jax.numpy as jnp
from jax import lax
from jax.experimental import pallas as pl
from jax.experimental.pallas import tpu as pltpu

F32 = jnp.float32
BF16 = jnp.bfloat16

HEAD_DIM = 64
BLOCK = 128
LANES = 128
LRU_WIDTH = 1024
LRU_BLOCKS = 16
LRU_CONV = 4
LRU_C = 8.0
LRU_CHUNK = 256
SWA_HEADS = 16
SWA_KV_HEADS = 4
SWA_WINDOW = 128
S5_WIDTH = 768
S5_GROUP = 16
S5_GROUPS = S5_WIDTH // S5_GROUP
S5_STATE = 64
S5_GPB = LANES // S5_GROUP
S5_NBLK = S5_WIDTH // LANES
S5_HALF = S5_GPB * S5_STATE
DIL_CONFIGS = ((128, 1), (512, 4), (2048, 16))
DIL_HEADS = 8
DIL_KV_HEADS = 4
D_FF = 5632
FFN_CONV = 3
FFN_COL_TILE = 512
FFN_TILES_PER_STEP = 2
ROW_TILE_STEPS = 32
LRU_HALO_STEPS = 4
ATTN_TQ = 2 * BLOCK
DEPTH = 2
ALPHA = (2 * DEPTH) ** 0.25
LN_EPS = 1e-5
MASK_BIAS = -1e30
ATTN_SEQS = 16

V7X_VMEM_LIMIT_BYTES = 56 * 1024 * 1024


def _params(*sem):
    return pltpu.CompilerParams(dimension_semantics=sem, vmem_limit_bytes=V7X_VMEM_LIMIT_BYTES)


def _resident(shape):
    nd = len(shape)
    return pl.BlockSpec(shape, lambda *_: (0,) * nd, pipeline_mode=pl.Buffered(1))


def _sigmoid(x):
    return 0.5 + 0.5 * jnp.tanh(0.5 * x)


def _gelu_tanh(x):
    return 0.5 * x * (1.0 + jnp.tanh(math.sqrt(2.0 / math.pi) * (x + 0.044715 * (x * x * x))))


def _layer_norm(y, g, b):
    mu = jnp.mean(y, axis=-1, keepdims=True)
    yc = y - mu
    var = jnp.mean(yc * yc, axis=-1, keepdims=True)
    return yc * lax.rsqrt(var + LN_EPS) * g + b


def _dot(a, b):
    return jnp.dot(a, b, preferred_element_type=F32)


def _inproj_kernel(x_ref, *refs, nw, chunk, batch_major_in):
    w_refs, o_refs = refs[:nw], refs[nw:]
    if batch_major_in:
        nb, ts, d = x_ref.shape
        x = pltpu.einshape("bsd->sbd", x_ref[...]).reshape(ts * nb, d)
        o_refs[-1][...] = x
    else:
        x = x_ref[...]
    xb = x.astype(BF16)
    for w_ref, o_ref in zip(w_refs, o_refs):
        width = o_ref.shape[1]
        for c0 in range(0, width, chunk):
            c1 = min(c0 + chunk, width)
            o_ref[:, c0:c1] = _dot(xb, w_ref[:, c0:c1]).astype(o_ref.dtype)


def _inproj(x, ws, dtypes, tm, batch_major_in):
    outs = [(w.shape[1], dt) for w, dt in zip(ws, dtypes)]
    if batch_major_in:
        nb, seq, d = x.shape
        t = nb * seq
        x_spec = pl.BlockSpec((nb, tm // nb, d), lambda i: (0, i, 0))
        outs.append((d, F32))
    else:
        t, d = x.shape
        x_spec = pl.BlockSpec((tm, d), lambda i: (i, 0))
    return pl.pallas_call(
        functools.partial(_inproj_kernel, nw=len(ws), chunk=512, batch_major_in=batch_major_in),
        grid=(t // tm,),
        in_specs=[x_spec] + [_resident(w.shape) for w in ws],
        out_specs=[pl.BlockSpec((tm, wd), lambda i: (i, 0)) for wd, _ in outs],
        out_shape=[jax.ShapeDtypeStruct((t, wd), dt) for wd, dt in outs],
        compiler_params=_params("parallel"),
        name="inproj",
    )(x, *ws)


def _lru_kernel(xa_ref, halo_ref, ga_ref, cw_ref, cb_ref, wx_ref, bx_ref, wa_ref, ba_ref, l_ref,
                y_ref, ext_s, a_s, h_s, carry_s, *, nb, tm):
    i = pl.program_id(1)
    halo_rows = (LRU_CONV - 1) * nb

    @pl.when(i == 0)
    def _():
        carry_s[...] = jnp.zeros_like(carry_s)

    halo = halo_ref[halo_ref.shape[0] - halo_rows:, :]
    ext_s[0:halo_rows, :] = jnp.where(i > 0, halo, 0.0)
    ext_s[halo_rows:, :] = xa_ref[...]
    xc = cb_ref[...]
    for k in range(LRU_CONV):
        xc = xc + cw_ref[k:k + 1, :] * ext_s[k * nb:k * nb + tm, :]

    xcb = xc.astype(BF16)
    i_gate = _sigmoid(_dot(xcb, wx_ref[0]) + bx_ref[...])
    r_gate = _sigmoid(_dot(xcb, wa_ref[0]) + ba_ref[...])
    neg_l = -l_ref[...]
    softplus = jnp.maximum(neg_l, 0.0) + jnp.log1p(jnp.exp(-jnp.abs(neg_l)))
    log_a = (-LRU_C * softplus) * r_gate
    a = jnp.exp(log_a)
    a_s[...] = a
    om = 1.0 - a * a
    h_s[...] = jnp.where(om > 0.0, om * lax.rsqrt(om), 0.0) * (i_gate * xc)

    def step(t, h):
        r0 = pl.multiple_of(t * nb, nb)
        h = a_s[pl.ds(r0, nb), :] * h + h_s[pl.ds(r0, nb), :]
        h_s[pl.ds(r0, nb), :] = h
        return h

    carry_s[...] = lax.fori_loop(0, tm // nb, step, carry_s[...], unroll=4)
    y_ref[...] = (h_s[...] * _gelu_tanh(ga_ref[...])).astype(y_ref.dtype)


def _rg_lru(h_a, conv_w, conv_b, wx4, bx, wa4, ba, lam, nb, tm, halo_blk):
    t = h_a.shape[0]
    nc = LRU_WIDTH // LRU_CHUNK
    c = LRU_CHUNK
    row = lambda cc, i: (0, cc)
    return pl.pallas_call(
        functools.partial(_lru_kernel, nb=nb, tm=tm),
        grid=(nc, t // tm),
        in_specs=[
            pl.BlockSpec((tm, c), lambda cc, i: (i, cc)),
            pl.BlockSpec((halo_blk, c), lambda cc, i: (jnp.maximum(i * (tm // halo_blk) - 1, 0), cc)),
            pl.BlockSpec((tm, c), lambda cc, i: (i, nc + cc)),
            pl.BlockSpec((LRU_CONV, c), row),
            pl.BlockSpec((1, c), row),
            pl.BlockSpec((1, c, c), lambda cc, i: (cc, 0, 0)),
            pl.BlockSpec((1, c), row),
            pl.BlockSpec((1, c, c), lambda cc, i: (cc, 0, 0)),
            pl.BlockSpec((1, c), row),
            pl.BlockSpec((1, c), row),
        ],
        out_specs=pl.BlockSpec((tm, c), lambda cc, i: (i, cc)),
        out_shape=jax.ShapeDtypeStruct((t, LRU_WIDTH), BF16),
        scratch_shapes=[
            pltpu.VMEM((tm + (LRU_CONV - 1) * nb, c), F32),
            pltpu.VMEM((tm, c), F32),
            pltpu.VMEM((tm, c), F32),
            pltpu.VMEM((nb, c), F32),
        ],
        compiler_params=_params("parallel", "arbitrary"),
        name="rg_lru",
    )(h_a, h_a, h_a, conv_w, conv_b, wx4, bx, wa4, ba, lam)


def _pair_heads(w, n_kvh, g, axis):
    w = jnp.moveaxis(w, axis, -1)
    lead = w.shape[:-1]
    w = w.reshape(lead + (-1, n_kvh // 2, 2, g, HEAD_DIM))
    w = jnp.swapaxes(w, -3, -2)
    return jnp.moveaxis(w.reshape(lead + (-1,)), -1, axis)


def _attn_kernel(*refs, g, tq, max_dist, has_sinks, has_lse):
    refs = list(refs)
    sink_ref = refs.pop(0) if has_sinks else None
    q_ref, kc_ref, vc_ref, kp_ref, vp_ref, o_ref = refs[:6]
    refs = refs[6:]
    lse_ref = refs.pop(0) if has_lse else None
    q_s, k_s, v_s, o_s = refs[:4]
    l_s = refs[4] if has_lse else None
    i = pl.program_id(0)
    pair = pl.program_id(2)
    ns = q_ref.shape[1]
    nblk = tq // BLOCK
    half = LANES // 2
    nrb = 2 * g

    q_s[...] = pltpu.einshape("tnw->ntw", q_ref[...])
    k_s[:, 0:BLOCK, :] = pltpu.einshape("tnw->ntw", kp_ref[...])
    k_s[:, BLOCK:, :] = pltpu.einshape("tnw->ntw", kc_ref[...])
    v_s[:, 0:BLOCK, :] = pltpu.einshape("tnw->ntw", vp_ref[...])
    v_s[:, BLOCK:, :] = pltpu.einshape("tnw->ntw", vc_ref[...])

    lo = lax.broadcasted_iota(jnp.int32, (1, LANES), 1) < half
    qi = lax.broadcasted_iota(jnp.int32, (BLOCK, 2 * BLOCK), 0)
    kj = lax.broadcasted_iota(jnp.int32, (BLOCK, 2 * BLOCK), 1)
    dist = qi + BLOCK - kj
    band = (dist >= 0) & (dist <= max_dist)
    bias_any = jnp.where(band, 0.0, MASK_BIAS).astype(F32)
    bias_first = jnp.where(band & (kj >= BLOCK), 0.0, MASK_BIAS).astype(F32)
    ones = jnp.ones((2 * BLOCK, LANES), BF16)
    if has_sinks:
        sinks = [sink_ref[(2 * pair + (rb % 2)) * g + rb // 2] for rb in range(nrb)]

    def seq_body(n, carry):
        for jb in range(nblk):
            qrows = slice(jb * BLOCK, (jb + 1) * BLOCK)
            krows = slice(jb * BLOCK, (jb + 2) * BLOCK)
            parts = []
            for j in range(g):
                qj = q_s[n, qrows, j * LANES:(j + 1) * LANES]
                zero = jnp.zeros_like(qj)
                parts += [jnp.where(lo, qj, zero), jnp.where(lo, zero, qj)]
            lhs = jnp.concatenate(parts, axis=0)
            s = lax.dot_general(lhs, k_s[n, krows, :], (((1,), (1,)), ((), ())),
                                preferred_element_type=F32)
            bias = jnp.where(i * nblk + jb > 0, bias_any, bias_first)
            m, p = [], []
            for rb in range(nrb):
                s_rb = s[rb * BLOCK:(rb + 1) * BLOCK, :] + bias
                m_rb = jnp.max(s_rb, axis=-1, keepdims=True)
                if has_sinks:
                    m_rb = jnp.maximum(m_rb, sinks[rb])
                m.append(m_rb)
                p.append(jnp.exp(s_rb - m_rb).astype(BF16))
            vones = jnp.concatenate([v_s[n, krows, :], ones], axis=1)
            od = _dot(jnp.concatenate(p, axis=0), vones)
            den = []
            for rb in range(nrb):
                d_rb = od[rb * BLOCK:(rb + 1) * BLOCK, LANES:2 * LANES]
                if has_sinks:
                    d_rb = d_rb + jnp.exp(sinks[rb] - m[rb])
                den.append(d_rb)
            for j in range(g):
                ra, rb = 2 * j, 2 * j + 1
                val = jnp.where(lo, od[ra * BLOCK:(ra + 1) * BLOCK, 0:LANES],
                                od[rb * BLOCK:(rb + 1) * BLOCK, 0:LANES])
                dsel = jnp.where(lo, den[ra], den[rb])
                o_s[n, qrows, j * LANES:(j + 1) * LANES] = (val / dsel).astype(o_s.dtype)
                if has_lse:
                    l_s[n, qrows, j * LANES:(j + 1) * LANES] = (
                        jnp.where(lo, m[ra], m[rb]) + jnp.log(dsel))
        return carry

    lax.fori_loop(0, ns, seq_body, 0, unroll=min(ns, 8 // nblk))
    o_ref[...] = pltpu.einshape("ntw->tnw", o_s[...])
    if has_lse:
        lse_ref[...] = pltpu.einshape("ntw->tnw", l_s[...])


def _banded_attention(q3, kv3, *, q_blk0, g, n_kvh, max_dist, tq, sinks=None, has_lse=False,
                      out_dtype=BF16):
    length, nseq, _ = q3.shape
    ns = min(ATTN_SEQS, nseq)
    npair = n_kvh // 2
    qbw = g * LANES
    rpb = tq // BLOCK
    cur = lambda off: (lambda i, n, p: (i, n, off + p))
    prev = lambda off: (lambda i, n, p: (jnp.maximum(i * rpb - 1, 0), n, off + p))
    in_specs = [
        pl.BlockSpec((tq, ns, qbw), cur(q_blk0)),
        pl.BlockSpec((tq, ns, LANES), cur(0)),
        pl.BlockSpec((tq, ns, LANES), cur(npair)),
        pl.BlockSpec((BLOCK, ns, LANES), prev(0)),
        pl.BlockSpec((BLOCK, ns, LANES), prev(npair)),
    ]
    args = [q3, kv3, kv3, kv3, kv3]
    if sinks is not None:
        in_specs.insert(0, pl.BlockSpec(memory_space=pltpu.SMEM))
        args.insert(0, sinks)
    ow = n_kvh * g * HEAD_DIM
    out_shape = [jax.ShapeDtypeStruct((length, nseq, ow), out_dtype)]
    out_specs = [pl.BlockSpec((tq, ns, qbw), cur(0))]
    scratch = [
        pltpu.VMEM((ns, tq, qbw), BF16),
        pltpu.VMEM((ns, tq + BLOCK, LANES), BF16),
        pltpu.VMEM((ns, tq + BLOCK, LANES), BF16),
        pltpu.VMEM((ns, tq, qbw), out_dtype),
    ]
    if has_lse:
        out_shape.append(jax.ShapeDtypeStruct((length, nseq, ow), F32))
        out_specs.append(pl.BlockSpec((tq, ns, qbw), cur(0)))
        scratch.append(pltpu.VMEM((ns, tq, qbw), F32))
    return pl.pallas_call(
        functools.partial(_attn_kernel, g=g, tq=tq, max_dist=max_dist,
                          has_sinks=sinks is not None, has_lse=has_lse),
        grid=(length // tq, nseq // ns, npair),
        in_specs=in_specs,
        out_specs=out_specs,
        out_shape=out_shape,
        scratch_shapes=scratch,
        compiler_params=_params("parallel", "parallel", "parallel"),
        name="banded_attn",
    )(*args)


def _s5_kernel(u_ref, bd_ref, lre_ref, lim_ref, cd_ref, d_ref, gw_ref, gb_ref, y_ref,
               sb_s, st_s, *, nb, tm):
    hw = S5_HALF

    @pl.when(pl.program_id(0) == 0)
    def _():
        st_s[...] = jnp.zeros_like(st_s)

    zs = []
    for j in range(S5_NBLK):
        u = u_ref[:, j * LANES:(j + 1) * LANES]
        sb_s[j] = _dot(u.astype(BF16), bd_ref[j])
        lre = jnp.broadcast_to(lre_ref[j], (nb, hw))
        lim = jnp.broadcast_to(lim_ref[j], (nb, hw))
        sre = st_s[j, :, 0:hw]
        sim = st_s[j, :, hw:2 * hw]
        for t in range(tm // nb):
            rows = slice(t * nb, (t + 1) * nb)
            sre, sim = (lre * sre - lim * sim + sb_s[j, rows, 0:hw],
                        lre * sim + lim * sre + sb_s[j, rows, hw:2 * hw])
            sb_s[j, rows, 0:hw] = sre
            sb_s[j, rows, hw:2 * hw] = sim
        st_s[j, :, 0:hw] = sre
        st_s[j, :, hw:2 * hw] = sim
        cs = _dot(sb_s[j].astype(BF16), cd_ref[j])
        y = cs[:, 0:LANES] + cs[:, LANES:2 * LANES] + d_ref[:, j * LANES:(j + 1) * LANES] * u
        zs.append(_gelu_tanh(y))
    z = jnp.concatenate(zs, axis=1)
    gate = _sigmoid(_dot(z.astype(BF16), gw_ref[...]) + gb_ref[...])
    y_ref[...] = (z * gate).astype(y_ref.dtype)


def _s5(u, bd, lre, lim, cd, dsk, glu_w, glu_b, nb, tm):
    t = u.shape[0]
    return pl.pallas_call(
        functools.partial(_s5_kernel, nb=nb, tm=tm),
        grid=(t // tm,),
        in_specs=[pl.BlockSpec((tm, S5_WIDTH), lambda i: (i, 0))]
        + [_resident(a.shape) for a in (bd, lre, lim, cd, dsk, glu_w, glu_b)],
        out_specs=pl.BlockSpec((tm, S5_WIDTH), lambda i: (i, 0)),
        out_shape=jax.ShapeDtypeStruct((t, S5_WIDTH), BF16),
        scratch_shapes=[
            pltpu.VMEM((S5_NBLK, tm, 2 * S5_HALF), F32),
            pltpu.VMEM((S5_NBLK, nb, 2 * S5_HALF), F32),
        ],
        compiler_params=_params("arbitrary"),
        name="s5",
    )(u, bd, lre, lim, cd, dsk, glu_w, glu_b)


def _outproj_even_kernel(ya_ref, ob_ref, x_ref, w1_ref, w2_ref, g_ref, b_ref, o_ref, *, rows):
    for r0 in range(0, x_ref.shape[0], rows):
        rs = slice(r0, r0 + rows)
        mix = _dot(ya_ref[rs, :], w1_ref[...]) + _dot(ob_ref[rs, :], w2_ref[...])
        o_ref[rs, :] = _layer_norm(ALPHA * x_ref[rs, :] + mix, g_ref[...], b_ref[...])


def _outproj_odd_kernel(yc_ref, o0_ref, o1_ref, o2_ref, l0_ref, l1_ref, l2_ref, x_ref, w1_ref,
                        w2_ref, g_ref, b_ref, o_ref, *, rows):
    for r0 in range(0, x_ref.shape[0], rows):
        rs = slice(r0, r0 + rows)
        l0, l1, l2 = l0_ref[rs, :], l1_ref[rs, :], l2_ref[rs, :]
        m = jnp.maximum(jnp.maximum(l0, l1), l2)
        e0, e1, e2 = jnp.exp(l0 - m), jnp.exp(l1 - m), jnp.exp(l2 - m)
        yd = (e0 * o0_ref[rs, :] + e1 * o1_ref[rs, :] + e2 * o2_ref[rs, :]) / (e0 + e1 + e2)
        mix = _dot(yc_ref[rs, :], w1_ref[...]) + _dot(yd.astype(BF16), w2_ref[...])
        o_ref[rs, :] = _layer_norm(ALPHA * x_ref[rs, :] + mix, g_ref[...], b_ref[...])


def _outproj_ln(kernel_fn, acts, x, w1, w2, ln_g, ln_b, tm, name):
    t, d = x.shape
    row = lambda i: (i, 0)
    return pl.pallas_call(
        functools.partial(kernel_fn, rows=256),
        grid=(t // tm,),
        in_specs=[pl.BlockSpec((tm, a.shape[1]), row) for a in acts]
        + [pl.BlockSpec((tm, d), row), _resident(w1.shape), _resident(w2.shape),
           _resident((1, d)), _resident((1, d))],
        out_specs=pl.BlockSpec((tm, d), row),
        out_shape=jax.ShapeDtypeStruct((t, d), F32),
        compiler_params=_params("parallel"),
        name=name,
    )(*acts, x, w1, w2, ln_g, ln_b)


def _ffn_kernel(x_ref, *refs, nb, tm, nj, batch_major_out):
    per = FFN_TILES_PER_STEP
    tiles = [refs[5 * k:5 * k + 5] for k in range(per)]
    g_ref, b_ref, o_ref, xb_s, hg_s, hv_s, cg_s, cv_s, acc_s = refs[5 * per:]
    i = pl.program_id(0)
    j = pl.program_id(1)
    halo = (FFN_CONV - 1) * nb

    @pl.when(j == 0)
    def _():
        xb_s[...] = x_ref[...].astype(BF16)
        acc_s[...] = jnp.zeros_like(acc_s)

    def column_tile(jt, wg_ref, wv_ref, pg_ref, pv_ref, wd_ref):
        @pl.when(i == 0)
        def _():
            cg_s[jt] = jnp.zeros(cg_s.shape[1:], F32)
            cv_s[jt] = jnp.zeros(cv_s.shape[1:], F32)

        hg_s[0:halo, :] = cg_s[jt]
        hv_s[0:halo, :] = cv_s[jt]
        hg_s[halo:, :] = _dot(xb_s[...], wg_ref[...])
        hv_s[halo:, :] = _dot(xb_s[...], wv_ref[...])
        cg_s[jt] = hg_s[tm:tm + halo, :]
        cv_s[jt] = hv_s[tm:tm + halo, :]

        gate = pg_ref[FFN_CONV:FFN_CONV + 1, :]
        val = pv_ref[FFN_CONV:FFN_CONV + 1, :]
        for k in range(FFN_CONV):
            gate = gate + pg_ref[k:k + 1, :] * hg_s[k * nb:k * nb + tm, :]
            val = val + pv_ref[k:k + 1, :] * hv_s[k * nb:k * nb + tm, :]
        act = (gate * _sigmoid(gate) * val).astype(BF16)
        acc_s[...] += _dot(act, wd_ref[...])

    column_tile(per * j, *tiles[0])
    for k in range(1, per):
        @pl.when(per * j + k < nj)
        def _():
            column_tile(per * j + k, *tiles[k])

    @pl.when(j == pl.num_programs(1) - 1)
    def _():
        y = _layer_norm(ALPHA * x_ref[...] + acc_s[...], g_ref[...], b_ref[...])
        if batch_major_out:
            y = pltpu.einshape("sbd->bsd", y.reshape(tm // nb, nb, y.shape[1]))
        o_ref[...] = y


def _conv_ffn_ln(x, w_up, conv_w, conv_b, w_down, ln_g, ln_b, nb, tm, tf, batch_major_out):
    t, d = x.shape
    nj = D_FF // tf
    halo = (FFN_CONV - 1) * nb
    if batch_major_out:
        out_spec = pl.BlockSpec((nb, tm // nb, d), lambda i, j: (0, i, 0))
        out_shape = jax.ShapeDtypeStruct((nb, t // nb, d), F32)
    else:
        out_spec = pl.BlockSpec((tm, d), lambda i, j: (i, 0))
        out_shape = jax.ShapeDtypeStruct((t, d), F32)
    per = FFN_TILES_PER_STEP
    conv_p = jnp.concatenate([conv_w, conv_b], axis=0)
    tile_specs, tile_args = [], []
    for k in range(per):
        col = lambda j, k=k: jnp.minimum(per * j + k, nj - 1)
        tile_specs += [
            pl.BlockSpec((d, tf), lambda i, j, c=col: (0, c(j))),
            pl.BlockSpec((d, tf), lambda i, j, c=col: (0, nj + c(j))),
            pl.BlockSpec((FFN_CONV + 1, tf), lambda i, j, c=col: (0, c(j))),
            pl.BlockSpec((FFN_CONV + 1, tf), lambda i, j, c=col: (0, nj + c(j))),
            pl.BlockSpec((tf, d), lambda i, j, c=col: (c(j), 0)),
        ]
        tile_args += [w_up, w_up, conv_p, conv_p, w_down]
    return pl.pallas_call(
        functools.partial(_ffn_kernel, nb=nb, tm=tm, nj=nj, batch_major_out=batch_major_out),
        grid=(t // tm, pl.cdiv(nj, per)),
        in_specs=[pl.BlockSpec((tm, d), lambda i, j: (i, 0))] + tile_specs
        + [_resident(ln_g.shape), _resident(ln_b.shape)],
        out_specs=out_spec,
        out_shape=out_shape,
        scratch_shapes=[
            pltpu.VMEM((tm, d), BF16),
            pltpu.VMEM((tm + halo, tf), F32),
            pltpu.VMEM((tm + halo, tf), F32),
            pltpu.VMEM((nj, halo, tf), F32),
            pltpu.VMEM((nj, halo, tf), F32),
            pltpu.VMEM((tm, d), F32),
        ],
        compiler_params=_params("arbitrary", "arbitrary"),
        name="conv_ffn_ln",
    )(x, *tile_args, ln_g, ln_b)


def _row(v):
    return v.astype(F32).reshape(1, -1)


def _block_diag(w, per):
    n, c, d = w.shape
    eye = jnp.eye(per, dtype=w.dtype)
    out = jnp.einsum('qncd,nm->qncmd', w.reshape(n // per, per, c, d), eye)
    return out.reshape(n // per, per * c, per * d)


def _s5_matrices(a_re, a_im, log_dt, b_re, b_im, c_re, c_im):
    a_re, a_im = a_re.astype(F32), a_im.astype(F32)
    dt = jnp.exp(log_dt.astype(F32))[:, None]
    mag = jnp.exp(a_re * dt)
    lre = mag * jnp.cos(a_im * dt)
    lim = mag * jnp.sin(a_im * dt)
    den = a_re * a_re + a_im * a_im
    cre = ((lre - 1.0) * a_re + lim * a_im) / den
    cim = (lim * a_re - (lre - 1.0) * a_im) / den
    bb_re = cre[:, :, None] * b_re - cim[:, :, None] * b_im
    bb_im = cre[:, :, None] * b_im + cim[:, :, None] * b_re
    eye = jnp.eye(S5_GPB, dtype=F32)
    shp = (S5_NBLK, S5_GPB, S5_STATE, S5_GROUP)
    bd = jnp.concatenate([
        jnp.einsum('jgpc,gh->jgchp', m.reshape(shp), eye).reshape(S5_NBLK, LANES, S5_HALF)
        for m in (bb_re, bb_im)], axis=2)
    shc = (S5_NBLK, S5_GPB, S5_GROUP, S5_STATE)
    cd_re, cd_im = [
        jnp.einsum('jgcp,gh->jgphc', m.reshape(shc), eye).reshape(S5_NBLK, S5_HALF, LANES)
        for m in (c_re.astype(F32), -c_im.astype(F32))]
    zero = jnp.zeros_like(cd_re)
    cd = jnp.concatenate([jnp.concatenate([cd_re, zero], axis=2),
                          jnp.concatenate([zero, cd_im], axis=2)], axis=1)
    return (bd.astype(BF16), lre.reshape(S5_NBLK, 1, S5_HALF), lim.reshape(S5_NBLK, 1, S5_HALF),
            cd.astype(BF16))


def _even_layer(x, w_in, conv_w, conv_b, gx_w, gx_b, ga_w, ga_b, lru_l, sinks, w_out,
                ln1_g, ln1_b, ffn_up, ffn_conv_w, ffn_conv_b, ffn_down, ln2_g, ln2_b, *, tm):
    bsz, seq, d = x.shape
    t = bsz * seq
    g = SWA_HEADS // SWA_KV_HEADS
    qw = SWA_HEADS * HEAD_DIM
    kw = SWA_KV_HEADS * HEAD_DIM
    o_q = 2 * LRU_WIDTH
    w_q = _pair_heads(w_in[:, o_q:o_q + qw], SWA_KV_HEADS, g, 1) * HEAD_DIM ** -0.5
    ws = [w.astype(BF16) for w in (w_in[:, :o_q], w_q, w_in[:, o_q + qw:])]
    h_a, h_q, h_kv, x_tm = _inproj(x, ws, (F32, BF16, BF16), tm, True)

    per = LRU_CHUNK // (LRU_WIDTH // LRU_BLOCKS)
    ya = _rg_lru(h_a, conv_w.astype(F32), _row(conv_b),
                 _block_diag(gx_w.astype(F32), per).astype(BF16), _row(gx_b),
                 _block_diag(ga_w.astype(F32), per).astype(BF16), _row(ga_b), _row(lru_l),
                 bsz, min(2 * tm, t), LRU_HALO_STEPS * bsz)

    ob, = _banded_attention(
        h_q.reshape(seq, bsz, qw), h_kv.reshape(seq, bsz, 2 * kw), q_blk0=0, g=g,
        n_kvh=SWA_KV_HEADS, max_dist=SWA_WINDOW - 1, tq=ATTN_TQ, sinks=sinks.astype(F32))

    x1 = _outproj_ln(_outproj_even_kernel, [ya, ob.reshape(t, qw)], x_tm,
                     w_out[:LRU_WIDTH].astype(BF16),
                     _pair_heads(w_out[LRU_WIDTH:], SWA_KV_HEADS, g, 0).astype(BF16),
                     _row(ln1_g), _row(ln1_b), tm, "outproj_even")
    return _conv_ffn_ln(x1, ffn_up.astype(BF16), ffn_conv_w.astype(F32), _row(ffn_conv_b),
                        ffn_down.astype(BF16), _row(ln2_g), _row(ln2_b), bsz, tm, FFN_COL_TILE, False)


def _odd_layer(x, bsz, w_in, a_re, a_im, log_dt, b_re, b_im, c_re, c_im, d_skip, glu_w, glu_b,
               w_out, ln1_g, ln1_b, ffn_up, ffn_conv_w, ffn_conv_b, ffn_down, ln2_g, ln2_b, *, tm):
    t, d = x.shape
    seq = t // bsz
    ncfg = len(DIL_CONFIGS)
    g = DIL_HEADS // DIL_KV_HEADS
    qw = DIL_HEADS * HEAD_DIM
    kw = DIL_KV_HEADS * HEAD_DIM
    o_q = S5_WIDTH
    w_q = _pair_heads(w_in[:, o_q:o_q + ncfg * qw], DIL_KV_HEADS, g, 1) * HEAD_DIM ** -0.5
    ws = [w.astype(BF16) for w in (w_in[:, :o_q], w_q, w_in[:, o_q + ncfg * qw:])]
    u, h_q, h_kv = _inproj(x, ws, (F32, BF16, BF16), tm, False)

    bd, lre, lim, cd = _s5_matrices(a_re, a_im, log_dt, b_re, b_im, c_re, c_im)
    yc = _s5(u, bd, lre, lim, cd, _row(d_skip), glu_w.astype(BF16), _row(glu_b), bsz, tm)

    o_l = []
    for r, (window, dil) in enumerate(DIL_CONFIGS):
        length = seq // dil
        nseq = dil * bsz
        o, lse = _banded_attention(
            h_q.reshape(length, nseq, ncfg * qw), h_kv.reshape(length, nseq, 2 * kw),
            q_blk0=r * (DIL_KV_HEADS // 2), g=g, n_kvh=DIL_KV_HEADS, max_dist=window // dil,
            tq=min(length, ATTN_TQ), has_lse=True, out_dtype=F32)
        o_l.append((o.reshape(t, qw), lse.reshape(t, qw)))

    x1 = _outproj_ln(_outproj_odd_kernel, [yc] + [o for o, _ in o_l] + [l for _, l in o_l], x,
                     w_out[:S5_WIDTH].astype(BF16),
                     _pair_heads(w_out[S5_WIDTH:], DIL_KV_HEADS, g, 0).astype(BF16),
                     _row(ln1_g), _row(ln1_b), tm, "outproj_odd")
    return _conv_ffn_ln(x1, ffn_up.astype(BF16), ffn_conv_w.astype(F32), _row(ffn_conv_b),
                        ffn_down.astype(BF16), _row(ln2_g), _row(ln2_b), bsz, tm, FFN_COL_TILE, True)


def kernel(x, l0_w_in, l0_lru_conv_w, l0_lru_conv_b, l0_lru_gx_w, l0_lru_gx_b, l0_lru_ga_w, l0_lru_ga_b, l0_lru_L, l0_sinks, l0_w_out, l0_ln1_g, l0_ln1_b, l0_ffn_up, l0_ffn_conv_w, l0_ffn_conv_b, l0_ffn_down, l0_ln2_g, l0_ln2_b, l1_w_in, l1_s5_A_re, l1_s5_A_im, l1_s5_log_dt, l1_s5_B_re, l1_s5_B_im, l1_s5_C_re, l1_s5_C_im, l1_s5_D, l1_glu_w, l1_glu_b, l1_w_out, l1_ln1_g, l1_ln1_b, l1_ffn_up, l1_ffn_conv_w, l1_ffn_conv_b, l1_ffn_down, l1_ln2_g, l1_ln2_b):
    bsz, seq, d = x.shape
    assert seq % (DIL_CONFIGS[-1][1] * BLOCK) == 0 and bsz % 8 == 0
    tm = ROW_TILE_STEPS * bsz
    h = _even_layer(x, l0_w_in, l0_lru_conv_w, l0_lru_conv_b, l0_lru_gx_w, l0_lru_gx_b, l0_lru_ga_w,
                    l0_lru_ga_b, l0_lru_L, l0_sinks, l0_w_out, l0_ln1_g, l0_ln1_b, l0_ffn_up,
                    l0_ffn_conv_w, l0_ffn_conv_b, l0_ffn_down, l0_ln2_g, l0_ln2_b, tm=tm)
    return _odd_layer(h, bsz, l1_w_in, l1_s5_A_re, l1_s5_A_im, l1_s5_log_dt, l1_s5_B_re, l1_s5_B_im,
                      l1_s5_C_re, l1_s5_C_im, l1_s5_D, l1_glu_w, l1_glu_b, l1_w_out, l1_ln1_g, l1_ln1_b,
                      l1_ffn_up, l1_ffn_conv_w, l1_ffn_conv_b, l1_ffn_down, l1_ln2_g, l1_ln2_b, tm=tm)
```
